```python
import jax, jax.numpy as jnp
from jax import lax
import numpy as np

D_MODEL = 1024
BATCH = 8
SEQ = 2048
DEPTH = 4
DEC_BATCH = 128
DEC_SEQ = 4
PAST_LEN = 2048
PAGE_SIZE = 128

HEAD_DIM = 64
N_MIXERS = 4
D_MIX = D_MODEL
D_GRP = D_MIX // N_MIXERS
N_HEADS_GRP = D_GRP // HEAD_DIM
D_IN = 10 * D_GRP + N_HEADS_GRP
SGU_CHUNK = 128
MOBA_BLOCK = 256
MOBA_TOPK = 3
MOBA_Q_CHUNK = 64
FOX_Q_BLOCK = 128
FOX_BIAS_INIT = 3.0
CONV_W = 31
FFN_CONV_W = 3
D_FF = 11 * D_MODEL // 4
ROPE_THETA = 10000.0
EPS = 1e-6
NEG_INF = -1e30

kernel_name = "hymba_style_gmlp_moba_fox_conformer_decode_step"


def rms_norm(x, g):
    xf = x.astype(jnp.float32)
    y = xf * lax.rsqrt(jnp.mean(xf * xf, axis=-1, keepdims=True) + EPS)
    return (y * g.astype(jnp.float32)).astype(x.dtype)


def layer_norm(x, g, b):
    xf = x.astype(jnp.float32)
    mu = jnp.mean(xf, axis=-1, keepdims=True)
    xc = xf - mu
    var = jnp.mean(xc * xc, axis=-1, keepdims=True)
    return (xc * lax.rsqrt(var + EPS) * g.astype(jnp.float32) + b.astype(jnp.float32)).astype(x.dtype)


def rope(x, pos):
    half = HEAD_DIM // 2
    inv = ROPE_THETA ** (-jnp.arange(half, dtype=jnp.float32) / half)
    ang = pos.astype(jnp.float32)[:, None] * inv[None, :]
    cos = jnp.cos(ang)[None, :, None, :]
    sin = jnp.sin(ang)[None, :, None, :]
    xf = x.astype(jnp.float32)
    x1, x2 = xf[..., :half], xf[..., half:]
    return jnp.concatenate([x1 * cos - x2 * sin, x2 * cos + x1 * sin], axis=-1).astype(x.dtype)


def causal_dwconv(x, buf, w, b):
    xp = jnp.concatenate([buf.astype(x.dtype), x], axis=1)
    y = lax.conv_general_dilated(xp, w[:, None, :].astype(x.dtype), (1,), "VALID",
                                 dimension_numbers=("NWC", "WIO", "NWC"),
                                 feature_group_count=x.shape[-1])
    return y + b.astype(y.dtype), xp[:, xp.shape[1] - (w.shape[0] - 1):]


def gather_pages(pool, page_table):
    g = pool[page_table]
    return g.reshape((g.shape[0], g.shape[1] * g.shape[2]) + g.shape[3:])


def map_query_blocks(fn, q_pos, blk, *per_query):
    t = q_pos.shape[0]
    tb = min(t, blk)
    nb = t // tb
    xs = tuple(a.reshape((a.shape[0], nb, tb) + a.shape[2:]).swapaxes(0, 1) for a in per_query)
    out = lax.map(fn, (q_pos.reshape(nb, tb),) + xs)
    out = out.swapaxes(0, 1)
    return out.reshape((out.shape[0], t) + out.shape[3:])


def moba_attend(q, k, v, q_pos):
    b_, _, h_, hd = q.shape
    L = k.shape[1]
    nb = -(-L // MOBA_BLOCK)
    pad = nb * MOBA_BLOCK - L
    kb = jnp.pad(k, ((0, 0), (0, pad), (0, 0), (0, 0))).reshape(b_, nb, MOBA_BLOCK, h_, hd).transpose(0, 3, 1, 2, 4)
    vb = jnp.pad(v, ((0, 0), (0, pad), (0, 0), (0, 0))).reshape(b_, nb, MOBA_BLOCK, h_, hd).transpose(0, 3, 1, 2, 4)
    kmean = jnp.mean(kb.astype(jnp.float32), axis=3)
    n_sel = min(MOBA_TOPK, nb)
    bi = jnp.arange(b_)[:, None, None, None]
    hi = jnp.arange(h_)[None, :, None, None]
    blk_ids = jnp.arange(nb)
    slot_ids = jnp.arange(n_sel)
    off = jnp.arange(MOBA_BLOCK)
    scale = hd ** -0.5

    def chunk(args):
        pc, qc = args
        own = pc[0] // MOBA_BLOCK
        tc = qc.shape[1]
        qh = qc.transpose(0, 2, 1, 3)
        gate = jnp.einsum("bhtd,bhnd->bhtn", qh.astype(jnp.float32), kmean)
        gate = jnp.where(blk_ids < own, gate, NEG_INF)
        _, idx = lax.top_k(gate, n_sel)
        ks = kb[bi, hi, idx]
        vs = vb[bi, hi, idx]
        s_sel = jnp.einsum("bhtd,bhtjkd->bhtjk", qh, ks, preferred_element_type=jnp.float32) * scale
        s_sel = jnp.where((slot_ids < own)[:, None], s_sel, NEG_INF)
        ko = lax.dynamic_index_in_dim(kb, own, axis=2, keepdims=False)
        vo = lax.dynamic_index_in_dim(vb, own, axis=2, keepdims=False)
        s_own = jnp.einsum("bhtd,bhkd->bhtk", qh, ko, preferred_element_type=jnp.float32) * scale
        s_own = jnp.where((own * MOBA_BLOCK + off)[None, :] <= pc[:, None], s_own, NEG_INF)
        s = jnp.concatenate([s_sel.reshape(b_, h_, tc, n_sel * MOBA_BLOCK), s_own], axis=-1)
        p = jax.nn.softmax(s, axis=-1).astype(v.dtype)
        p_sel = p[..., :n_sel * MOBA_BLOCK].reshape(b_, h_, tc, n_sel, MOBA_BLOCK)
        p_own = p[..., n_sel * MOBA_BLOCK:]
        o = jnp.einsum("bhtjk,bhtjkd->bhtd", p_sel, vs) + jnp.einsum("bhtk,bhkd->bhtd", p_own, vo)
        return o.transpose(0, 2, 1, 3)

    return map_query_blocks(chunk, q_pos, MOBA_Q_CHUNK, q)


def fox_attend(q, k, v, c_q, c_k, q_pos):
    k_pos = jnp.arange(k.shape[1])
    scale = HEAD_DIM ** -0.5
    ck = c_k.transpose(0, 2, 1)

    def block(args):
        pb, qb, cqb = args
        s = jnp.einsum("bthd,bshd->bhts", qb, k, preferred_element_type=jnp.float32) * scale
        s = s + cqb.transpose(0, 2, 1)[..., None] - ck[:, :, None, :]
        s = jnp.where(k_pos[None, None, None, :] <= pb[None, None, :, None], s, NEG_INF)
        p = jax.nn.softmax(s, axis=-1).astype(v.dtype)
        return jnp.einsum("bhts,bshd->bthd", p, v)

    return map_query_blocks(block, q_pos, FOX_Q_BLOCK, q, c_q)


def trunk_layer(x, pos, p, past):
    b_, t_, _ = x.shape
    h = rms_norm(x, p["norm_mix"])
    z = h @ p["w_in"]
    sizes = [D_GRP] * 8 + [N_HEADS_GRP, 2 * D_GRP]
    cuts = [int(c) for c in np.cumsum(sizes)[:-1]]
    a_u, a_v, b_q, b_k, b_v, c_q, c_k, c_v, c_f, d_in = jnp.split(z, cuts, axis=-1)

    def heads(t):
        return t.reshape(b_, t_, N_HEADS_GRP, HEAD_DIM)

    u = jax.nn.gelu(a_u)
    sv = rms_norm(heads(jax.nn.gelu(a_v)), p["sgu_norm"])
    n = min(t_, SGU_CHUNK)
    vcn = sv.reshape(b_, t_ // n, n, N_HEADS_GRP, HEAD_DIM)
    w_s = jnp.tril(p["sgu_w"][:, :n, :n])
    mixed = jnp.einsum("gts,bcsgd->bctgd", w_s, vcn) + p["sgu_b"][:, :n].T[:, :, None]
    out_a = u * mixed.reshape(b_, t_, D_GRP)

    qb = rope(rms_norm(heads(b_q), p["moba_qn"]), pos)
    kb = rope(rms_norm(heads(b_k), p["moba_kn"]), pos)
    vb = heads(b_v)
    if past is None:
        kb_all, vb_all = kb, vb
    else:
        kb_all = jnp.concatenate([past["moba_k"].astype(kb.dtype), kb], axis=1)
        vb_all = jnp.concatenate([past["moba_v"].astype(vb.dtype), vb], axis=1)
    out_b = moba_attend(qb, kb_all, vb_all, pos).reshape(b_, t_, D_GRP)

    qc = rms_norm(heads(c_q), p["fox_qn"])
    kc = rms_norm(heads(c_k), p["fox_kn"])
    vc = heads(c_v)
    logf = jax.nn.log_sigmoid(c_f.astype(jnp.float32) + p["fox_fb"].astype(jnp.float32))
    if past is None:
        kc_all, vc_all, logf_all = kc, vc, logf
    else:
        kc_all = jnp.concatenate([past["fox_k"].astype(kc.dtype), kc], axis=1)
        vc_all = jnp.concatenate([past["fox_v"].astype(vc.dtype), vc], axis=1)
        logf_all = jnp.concatenate([past["fox_logf"].astype(jnp.float32), logf], axis=1)
    cum = jnp.cumsum(logf_all, axis=1)
    out_c = fox_attend(qc, kc_all, vc_all, cum[:, cum.shape[1] - t_:], cum, pos).reshape(b_, t_, D_GRP)

    d_a, d_g = jnp.split(d_in, 2, axis=-1)
    glu = d_a * jax.nn.sigmoid(d_g)
    buf = jnp.zeros((b_, CONV_W - 1, D_GRP), glu.dtype) if past is None else past["conv"]
    dconv, conv_tail = causal_dwconv(glu, buf, p["conv_w"], p["conv_b"])
    out_d = jax.nn.silu(layer_norm(dconv, p["conv_ln_g"], p["conv_ln_b"]))

    x = x + jnp.concatenate([out_a, out_b, out_c, out_d], axis=-1) @ p["w_out"]

    hf = rms_norm(x, p["norm_ffn"])
    up = hf @ p["w_up"]
    fbuf = jnp.zeros((b_, FFN_CONV_W - 1, 2 * D_FF), up.dtype) if past is None else past["ffn_conv"]
    upc, ffn_tail = causal_dwconv(up, fbuf, p["ffn_conv_w"], p["ffn_conv_b"])
    g, uu = jnp.split(upc, 2, axis=-1)
    x = x + (jax.nn.silu(g) * uu) @ p["w_down"]

    new = {"moba_k": kb, "moba_v": vb, "fox_k": kc, "fox_v": vc, "fox_logf": logf,
           "conv": conv_tail, "ffn_conv": ffn_tail, "sgu_v": sv.reshape(b_, t_, D_GRP)}
    return x, new


def setup_inputs(seed: int = 0) -> dict:
    key = jax.random.key(seed)
    ks = jax.random.split(key, 32)
    f32 = jnp.float32
    n_pages = PAST_LEN // PAGE_SIZE
    n_used = DEC_BATCH * n_pages
    n_phys = n_used + max(1, n_used // 4)

    def nrm(k, shape, s=1.0):
        return jax.random.normal(k, shape, f32) * s

    def gain(k, shape):
        return 1.0 + 0.05 * jax.random.normal(k, shape, f32)

    pool_kv = (DEPTH, n_phys, PAGE_SIZE, N_HEADS_GRP, HEAD_DIM)
    page_table = jax.random.permutation(ks[7], n_phys)[:n_used].reshape(DEC_BATCH, n_pages).astype(jnp.int32)
    return {
        "x_prompt": nrm(ks[0], (BATCH, SEQ, D_MODEL)),
        "x_sample": nrm(ks[1], (DEC_BATCH, DEC_SEQ, D_MODEL)),
        "cache_moba_k": nrm(ks[2], pool_kv),
        "cache_moba_v": nrm(ks[3], pool_kv),
        "cache_fox_k": nrm(ks[4], pool_kv),
        "cache_fox_v": nrm(ks[5], pool_kv),
        "cache_fox_logf": jax.nn.log_sigmoid(FOX_BIAS_INIT + nrm(ks[6], (DEPTH, n_phys, PAGE_SIZE, N_HEADS_GRP))),
        "state_conv": nrm(ks[8], (DEPTH, DEC_BATCH, CONV_W - 1, D_GRP), 0.5),
        "state_ffn_conv": nrm(ks[9], (DEPTH, DEC_BATCH, FFN_CONV_W - 1, 2 * D_FF), 0.5),
        "page_table": page_table,
        "norm_mix": gain(ks[10], (DEPTH, D_MODEL)),
        "w_in": nrm(ks[11], (DEPTH, D_MODEL, D_IN), D_MODEL ** -0.5),
        "sgu_norm": gain(ks[12], (DEPTH, N_HEADS_GRP, HEAD_DIM)),
        "sgu_w": nrm(ks[13], (DEPTH, N_HEADS_GRP, SGU_CHUNK, SGU_CHUNK), SGU_CHUNK ** -0.5),
        "sgu_b": 1.0 + 0.1 * nrm(ks[14], (DEPTH, N_HEADS_GRP, SGU_CHUNK)),
        "moba_qn": gain(ks[15], (DEPTH, HEAD_DIM)),
        "moba_kn": gain(ks[16], (DEPTH, HEAD_DIM)),
        "fox_qn": gain(ks[17], (DEPTH, HEAD_DIM)),
        "fox_kn": gain(ks[18], (DEPTH, HEAD_DIM)),
        "fox_fb": FOX_BIAS_INIT + 0.1 * nrm(ks[19], (DEPTH, N_HEADS_GRP)),
        "conv_w": nrm(ks[20], (DEPTH, CONV_W, D_GRP), CONV_W ** -0.5),
        "conv_b": 0.02 * nrm(ks[21], (DEPTH, D_GRP)),
        "conv_ln_g": gain(ks[22], (DEPTH, D_GRP)),
        "conv_ln_b": 0.02 * nrm(ks[23], (DEPTH, D_GRP)),
        "w_out": nrm(ks[24], (DEPTH, D_MIX, D_MODEL), D_MIX ** -0.5),
        "norm_ffn": gain(ks[25], (DEPTH, D_MODEL)),
        "w_up": nrm(ks[26], (DEPTH, D_MODEL, 2 * D_FF), D_MODEL ** -0.5),
        "ffn_conv_w": nrm(ks[27], (DEPTH, FFN_CONV_W, 2 * D_FF), FFN_CONV_W ** -0.5),
        "ffn_conv_b": 0.02 * nrm(ks[28], (DEPTH, 2 * D_FF)),
        "w_down": nrm(ks[29], (DEPTH, D_FF, D_MODEL), D_FF ** -0.5),
    }


def reference(x_prompt, x_sample, cache_moba_k, cache_moba_v, cache_fox_k, cache_fox_v, cache_fox_logf,
              state_conv, state_ffn_conv, page_table, norm_mix, w_in, sgu_norm, sgu_w, sgu_b,
              moba_qn, moba_kn, fox_qn, fox_kn, fox_fb, conv_w, conv_b, conv_ln_g, conv_ln_b, w_out,
              norm_ffn, w_up, ffn_conv_w, ffn_conv_b, w_down):
    past_len = page_table.shape[1] * PAGE_SIZE
    pos_p = jnp.arange(x_prompt.shape[1], dtype=jnp.int32)
    pos_s = past_len + jnp.arange(x_sample.shape[1], dtype=jnp.int32)
    keys_p = ("moba_k", "moba_v", "fox_k", "fox_v", "fox_logf", "conv", "ffn_conv")
    keys_s = keys_p + ("sgu_v",)
    new_p = {k: [] for k in keys_p}
    new_s = {k: [] for k in keys_s}
    xp, xs = x_prompt, x_sample
    for l in range(DEPTH):
        p = {"norm_mix": norm_mix[l], "w_in": w_in[l], "sgu_norm": sgu_norm[l], "sgu_w": sgu_w[l],
             "sgu_b": sgu_b[l], "moba_qn": moba_qn[l], "moba_kn": moba_kn[l], "fox_qn": fox_qn[l],
             "fox_kn": fox_kn[l], "fox_fb": fox_fb[l], "conv_w": conv_w[l], "conv_b": conv_b[l],
             "conv_ln_g": conv_ln_g[l], "conv_ln_b": conv_ln_b[l], "w_out": w_out[l],
             "norm_ffn": norm_ffn[l], "w_up": w_up[l], "ffn_conv_w": ffn_conv_w[l],
             "ffn_conv_b": ffn_conv_b[l], "w_down": w_down[l]}
        past = {"moba_k": gather_pages(cache_moba_k[l], page_table),
                "moba_v": gather_pages(cache_moba_v[l], page_table),
                "fox_k": gather_pages(cache_fox_k[l], page_table),
                "fox_v": gather_pages(cache_fox_v[l], page_table),
                "fox_logf": gather_pages(cache_fox_logf[l], page_table),
                "conv": state_conv[l], "ffn_conv": state_ffn_conv[l]}
        xp, sp = trunk_layer(xp, pos_p, p, None)
        xs, ss = trunk_layer(xs, pos_s, p, past)
        for k in keys_p:
            new_p[k].append(sp[k])
        for k in keys_s:
            new_s[k].append(ss[k])
    P = {k: jnp.stack(v) for k, v in new_p.items()}
    S = {k: jnp.stack(v) for k, v in new_s.items()}
    return (xp, xs,
            P["moba_k"], P["moba_v"], P["fox_k"], P["fox_v"], P["fox_logf"], P["conv"], P["ffn_conv"],
            S["moba_k"], S["moba_v"], S["fox_k"], S["fox_v"], S["fox_logf"], S["conv"], S["ffn_conv"],
            S["sgu_v"])
```

```python
import functools

import numpy as np
import jax
import jax.numpy as jnp
from jax import lax
from jax.experimental import pallas as pl
from jax.experimental.pallas import tpu as pltpu

F32 = jnp.float32
BF16 = jnp.bfloat16
HIGHEST = lax.Precision.HIGHEST

HEAD_DIM = 64
N_HEADS = 4
D_GRP = N_HEADS * HEAD_DIM
SGU_CHUNK = 128
MOBA_BLOCK = 256
MOBA_TOPK = 3
CONV_W = 31
FFN_CONV_W = 3
ROPE_THETA = 10000.0
EPS = 1e-6
NEG_INF = -1e30
LANES = 128
Q_SCALE = HEAD_DIM ** -0.5

TOKEN_TILE = 256
ATT_BLOCK = 256
FF_CHUNK = 256
CONV_ROWS = 64
VMEM_LIMIT = 52 * 1024 * 1024


def _gelu(x):
    return 0.5 * x * (1.0 + jnp.tanh(0.7978845608028654 * (x + 0.044715 * (x * x * x))))


def _log_sigmoid(x):
    return jnp.minimum(x, 0.0) - jnp.log1p(jnp.exp(-jnp.abs(x)))


def _rms(x, gain):
    ms = jnp.mean(x * x, axis=-1, keepdims=True)
    return x * lax.rsqrt(ms + EPS) * gain


def _head_rms(y, gain, bd):
    sq = y * y
    hi = sq.astype(BF16)
    lo = (sq - hi.astype(F32)).astype(BF16)
    ms = jnp.dot(hi, bd, preferred_element_type=F32) + jnp.dot(lo, bd, preferred_element_type=F32)
    return y * lax.rsqrt(ms + EPS) * gain


def _rope(y, cos, sin_signed):
    lane = lax.broadcasted_iota(jnp.int32, y.shape, 1)
    first = (lane % HEAD_DIM) < (HEAD_DIM // 2)
    partner = jnp.where(first, pltpu.roll(y, D_GRP - HEAD_DIM // 2, 1), pltpu.roll(y, HEAD_DIM // 2, 1))
    return y * cos + partner * sin_signed


def _head_of_lane(shape):
    return lax.broadcasted_iota(jnp.int32, shape, len(shape) - 1) // HEAD_DIM


def _in_groups(h, w_ref, bd, gains, fb, cos, sin_s):
    def grp(g, width=D_GRP):
        return jnp.dot(h, w_ref[:, g * D_GRP:g * D_GRP + width], preferred_element_type=F32)

    u = _gelu(grp(0))
    sv = _head_rms(_gelu(grp(1)), gains[0:1], bd)
    qb = _rope(_head_rms(grp(2), gains[1:2], bd), cos, sin_s) * Q_SCALE
    kb = _rope(_head_rms(grp(3), gains[2:3], bd), cos, sin_s)
    vb = grp(4)
    qc = _head_rms(grp(5), gains[3:4], bd) * Q_SCALE
    kc = _head_rms(grp(6), gains[4:5], bd)
    vc = grp(7)
    glu = grp(8) * jax.nn.sigmoid(grp(9))
    logf = _log_sigmoid(grp(10, LANES) + fb)
    return u, sv, qb, kb, vb, qc, kc, vc, logf, glu


def _layer_norm_silu(y, g, b):
    mu = jnp.mean(y, axis=-1, keepdims=True)
    yc = y - mu
    var = jnp.mean(yc * yc, axis=-1, keepdims=True)
    z = yc * lax.rsqrt(var + EPS) * g + b
    return z * jax.nn.sigmoid(z)


def _inproj_prompt_kernel(x_ref, gmix_ref, w_ref, bd_ref, gains_ref, fb_ref, cos_ref, sin_ref, wmix_ref, bmix_ref,
                          oa_ref, qb_ref, kbT_ref, vbT_ref, qc_ref, kcT_ref, vcT_ref, lfT_ref, glu_ref):
    h = _rms(x_ref[...], gmix_ref[...]).astype(BF16)
    u, sv, qb, kb, vb, qc, kc, vc, logf, glu = _in_groups(
        h, w_ref, bd_ref[...], gains_ref[...], fb_ref[...], cos_ref[...], sin_ref[...])
    qb_ref[...] = qb
    qc_ref[...] = qc
    glu_ref[...] = glu
    kbT_ref[...] = kb.T
    vbT_ref[...] = vb.T
    kcT_ref[...] = kc.T
    vcT_ref[...] = vc.T
    lfT_ref[...] = logf.T[0:N_HEADS, :]

    n = SGU_CHUNK
    row = lax.broadcasted_iota(jnp.int32, (n, n), 0)
    col = lax.broadcasted_iota(jnp.int32, (n, n), 1)
    head = _head_of_lane((n, D_GRP))
    svb = sv.astype(BF16)
    for c in range(x_ref.shape[0] // n):
        svc = svb[c * n:(c + 1) * n]
        mixed = bmix_ref[...]
        for g in range(N_HEADS):
            wg = jnp.where(row >= col, wmix_ref[g], 0.0).astype(BF16)
            mixed = mixed + jnp.where(head == g, jnp.dot(wg, svc, preferred_element_type=F32), 0.0)
        oa_ref[c * n:(c + 1) * n, :] = u[c * n:(c + 1) * n] * mixed


def _inproj_prompt(x, lp, consts, l):
    B, T, D = x.shape
    tm = TOKEN_TILE
    nt = T // tm
    din = lp["w_in"].shape[-1]
    tok = lambda w: pl.BlockSpec((None, tm, w), lambda b, i: (b, i, 0))
    chan = lambda r: pl.BlockSpec((None, r, tm), lambda b, i: (b, 0, i))
    layer = lambda *s: pl.BlockSpec((None,) + s, lambda b, i: (l,) + (0,) * len(s))
    const = lambda *s: pl.BlockSpec(s, lambda b, i: (0,) * len(s))
    outs = pl.pallas_call(
        _inproj_prompt_kernel,
        grid=(B, nt),
        in_specs=[tok(D), layer(1, D), layer(D, din), const(D_GRP, D_GRP), layer(8, D_GRP), layer(1, LANES),
                  pl.BlockSpec((tm, D_GRP), lambda b, i: (i, 0)), pl.BlockSpec((tm, D_GRP), lambda b, i: (i, 0)),
                  layer(N_HEADS, SGU_CHUNK, SGU_CHUNK), layer(SGU_CHUNK, D_GRP)],
        out_specs=[tok(D_GRP), tok(D_GRP), chan(D_GRP), chan(D_GRP), tok(D_GRP), chan(D_GRP), chan(D_GRP),
                   chan(N_HEADS), tok(D_GRP)],
        out_shape=[jax.ShapeDtypeStruct((B, T, D_GRP), F32), jax.ShapeDtypeStruct((B, T, D_GRP), F32),
                   jax.ShapeDtypeStruct((B, D_GRP, T), F32), jax.ShapeDtypeStruct((B, D_GRP, T), F32),
                   jax.ShapeDtypeStruct((B, T, D_GRP), F32), jax.ShapeDtypeStruct((B, D_GRP, T), F32),
                   jax.ShapeDtypeStruct((B, D_GRP, T), F32), jax.ShapeDtypeStruct((B, N_HEADS, T), F32),
                   jax.ShapeDtypeStruct((B, T, D_GRP), F32)],
        compiler_params=pltpu.CompilerParams(dimension_semantics=("arbitrary", "arbitrary"),
                                             vmem_limit_bytes=VMEM_LIMIT),
        name="inproj_prompt",
    )(x, lp["norm_mix"], lp["w_in"], consts["bd"], lp["gains"], lp["fb"], consts["cos_p"], consts["sin_p"],
      lp["sgu_w"], lp["sgu_bmix"])
    return outs


def _pair_tables(nq):
    qi, kj = [], []
    for i in range(nq):
        qi.append(i)
        kj.append(i)
        for j in range(i):
            qi.append(i)
            kj.append(j)
    return np.asarray(qi, np.int32), np.asarray(kj, np.int32)


def _attn_prompt_kernel(qi_ref, kj_ref, q_ref, kT_ref, vT_ref, lfT_ref, o_ref,
                        m_scr, l_scr, acc_scr, aux_scr, col_scr, sel_scr, *, fox):
    s_idx = pl.program_id(1)
    i = qi_ref[s_idx]
    j = kj_ref[s_idx]
    tq = q_ref.shape[0]
    tk = kT_ref.shape[1]
    diag = j == i
    last = jnp.logical_or(j == i - 1, i == 0)

    lane_q = lax.broadcasted_iota(jnp.int32, (tq, LANES), 1)

    if fox:
        @pl.when(s_idx == 0)
        def _():
            aux_scr[...] = jnp.zeros_like(aux_scr)
            r = lax.broadcasted_iota(jnp.int32, (LANES, LANES), 0)
            c = lax.broadcasted_iota(jnp.int32, (LANES, LANES), 1)
            upper = jnp.where(r <= c, 1.0, 0.0).astype(F32)
            carry = jnp.zeros((N_HEADS, 1), F32)
            per_blk = tk // LANES
            for cc in range(lfT_ref.shape[1] // LANES):
                w = jnp.dot(lfT_ref[:, cc * LANES:(cc + 1) * LANES], upper, precision=HIGHEST,
                            preferred_element_type=F32) + carry
                aux_scr[cc // per_blk, 0:N_HEADS, (cc % per_blk) * LANES:(cc % per_blk + 1) * LANES] = w
                carry = w[:, LANES - 1:LANES]
    else:
        @pl.when(s_idx == 0)
        def _():
            aux_scr[...] = jnp.zeros_like(aux_scr)

    q = q_ref[...]
    head = _head_of_lane((tq, D_GRP))

    @pl.when(diag)
    def _():
        m_scr[...] = jnp.full_like(m_scr, NEG_INF)
        l_scr[...] = jnp.zeros_like(l_scr)
        acc_scr[...] = jnp.zeros_like(acc_scr)
        if fox:
            r = lax.broadcasted_iota(jnp.int32, (tq, tq), 0)
            c = lax.broadcasted_iota(jnp.int32, (tq, tq), 1)
            eye = jnp.where(r == c, 1.0, 0.0).astype(F32)
            col_scr[...] = lax.dot_general(eye, aux_scr[i], (((1,), (1,)), ((), ())), precision=HIGHEST,
                                           preferred_element_type=F32)
        else:
            kmean = jnp.mean(kT_ref[...], axis=1, keepdims=True)
            lane_k = lax.broadcasted_iota(jnp.int32, (D_GRP, LANES), 1)
            aux_scr[...] = jnp.where(lane_k == i, kmean, aux_scr[...])
            km = aux_scr[...]
            for h in range(N_HEADS):
                qm = jnp.where(head == h, q, 0.0)
                gate = jnp.dot(qm, km, precision=HIGHEST, preferred_element_type=F32)
                rank = jnp.zeros((tq, LANES), F32)
                for mblk in range(LANES // 16):
                    gm = gate[:, mblk:mblk + 1]
                    ahead = (gm > gate) | ((gm == gate) & (mblk < lane_q))
                    rank = rank + jnp.where(ahead & (mblk < i), 1.0, 0.0)
                picked = ((rank < MOBA_TOPK) & (lane_q < i)) | (lane_q == i)
                sel_scr[h] = jnp.where(picked, 1.0, 0.0)

    kT = kT_ref[...].astype(BF16)
    vT = vT_ref[...].astype(BF16)
    row_pos = i * tq + lax.broadcasted_iota(jnp.int32, (tq, tk), 0)
    col_pos = j * tk + lax.broadcasted_iota(jnp.int32, (tq, tk), 1)
    causal = row_pos >= col_pos
    for h in range(N_HEADS):
        qm = jnp.where(head == h, q, 0.0).astype(BF16)
        s = jnp.dot(qm, kT, preferred_element_type=F32)
        if fox:
            s = s + col_scr[:, h:h + 1] - aux_scr[j, h:h + 1, :]
            keep = causal
        else:
            keep = causal & (jnp.sum(jnp.where(lane_q == j, sel_scr[h], 0.0), axis=1, keepdims=True) > 0.5)
        s = jnp.where(keep, s, NEG_INF)
        m_old = m_scr[h]
        m_new = jnp.maximum(m_old, jnp.max(s, axis=1, keepdims=True))
        alpha = jnp.exp(m_old - m_new)
        p = jnp.exp(s - m_new)
        l_scr[h] = alpha * l_scr[h] + jnp.sum(p, axis=1, keepdims=True)
        m_scr[h] = m_new
        pv = lax.dot_general(p.astype(BF16), vT, (((1,), (1,)), ((), ())), preferred_element_type=F32)
        acc_scr[...] = jnp.where(head == h, acc_scr[...] * alpha + pv, acc_scr[...])

    @pl.when(last)
    def _():
        out = jnp.zeros((tq, D_GRP), F32)
        for h in range(N_HEADS):
            out = jnp.where(head == h, acc_scr[...] / l_scr[h], out)
        o_ref[...] = out


def _attn_prompt(q, kT, vT, lfT, fox):
    B, T, _ = q.shape
    tq = ATT_BLOCK
    nq = T // tq
    assert nq - 1 <= LANES // 16
    qi, kj = _pair_tables(nq)
    aux = pltpu.VMEM((nq, 8, tq), F32) if fox else pltpu.VMEM((D_GRP, LANES), F32)
    grid_spec = pltpu.PrefetchScalarGridSpec(
        num_scalar_prefetch=2,
        grid=(B, len(qi)),
        in_specs=[pl.BlockSpec((None, tq, D_GRP), lambda b, s, qi, kj: (b, qi[s], 0)),
                  pl.BlockSpec((None, D_GRP, tq), lambda b, s, qi, kj: (b, 0, kj[s])),
                  pl.BlockSpec((None, D_GRP, tq), lambda b, s, qi, kj: (b, 0, kj[s])),
                  pl.BlockSpec((None, N_HEADS, T), lambda b, s, qi, kj: (b, 0, 0))],
        out_specs=pl.BlockSpec((None, tq, D_GRP), lambda b, s, qi, kj: (b, qi[s], 0)),
        scratch_shapes=[pltpu.VMEM((N_HEADS, tq, 1), F32), pltpu.VMEM((N_HEADS, tq, 1), F32),
                        pltpu.VMEM((tq, D_GRP), F32), aux, pltpu.VMEM((tq, 8), F32),
                        pltpu.VMEM((N_HEADS, tq, LANES), F32)],
    )
    return pl.pallas_call(
        functools.partial(_attn_prompt_kernel, fox=fox),
        grid_spec=grid_spec,
        out_shape=jax.ShapeDtypeStruct((B, T, D_GRP), F32),
        compiler_params=pltpu.CompilerParams(dimension_semantics=("arbitrary", "arbitrary"),
                                             vmem_limit_bytes=VMEM_LIMIT),
        name="fox_prompt" if fox else "moba_prompt",
    )(jnp.asarray(qi), jnp.asarray(kj), q, kT, vT, lfT)


def _conv_prompt_kernel(x_ref, w_ref, b_ref, g_ref, beta_ref, o_ref, tail_ref, s_scr):
    T = x_ref.shape[0]
    pad = 32
    s_scr[0:pad, :] = jnp.zeros((pad, D_GRP), F32)
    s_scr[pad:pad + T, :] = x_ref[...]
    tail_ref[...] = s_scr[pad + T - (CONV_W - 1):pad + T, :]
    w = w_ref[...]
    R = CONV_ROWS

    def chunk(c, _):
        r0 = pl.multiple_of(c * R, R)
        win = s_scr[pl.ds(r0, R + pad), :]
        acc = jnp.zeros((R, D_GRP), F32) + b_ref[...]
        for tap in range(CONV_W):
            off = tap + pad - (CONV_W - 1)
            acc = acc + w[tap:tap + 1, :] * win[off:off + R, :]
        o_ref[pl.ds(r0, R), :] = _layer_norm_silu(acc, g_ref[...], beta_ref[...])
        return 0

    lax.fori_loop(0, T // R, chunk, 0)


def _conv_prompt(glu, lp, l):
    B, T, _ = glu.shape
    layer = lambda *s: pl.BlockSpec((None,) + s, lambda b: (l,) + (0,) * len(s))
    return pl.pallas_call(
        _conv_prompt_kernel,
        grid=(B,),
        in_specs=[pl.BlockSpec((None, T, D_GRP), lambda b: (b, 0, 0)), layer(32, D_GRP), layer(1, D_GRP),
                  layer(1, D_GRP), layer(1, D_GRP)],
        out_specs=[pl.BlockSpec((None, T, D_GRP), lambda b: (b, 0, 0)),
                   pl.BlockSpec((None, CONV_W - 1, D_GRP), lambda b: (b, 0, 0))],
        out_shape=[jax.ShapeDtypeStruct((B, T, D_GRP), F32), jax.ShapeDtypeStruct((B, CONV_W - 1, D_GRP), F32)],
        scratch_shapes=[pltpu.VMEM((T + 32, D_GRP), F32)],
        compiler_params=pltpu.CompilerParams(dimension_semantics=("arbitrary",), vmem_limit_bytes=VMEM_LIMIT),
        name="conv_prompt",
    )(glu, lp["conv_w"], lp["conv_b"], lp["conv_ln_g"], lp["conv_ln_b"])


def _ffn_prompt_kernel(x_ref, oa_ref, ob_ref, oc_ref, od_ref, wo_ref, gffn_ref, wup_ref, cw_ref, cb_ref, wdn_ref,
                       y_ref, tail_ref, s_scr):
    t_idx = pl.program_id(1)
    tm = x_ref.shape[0]
    dff = wdn_ref.shape[0]
    nchunk = dff // FF_CHUNK

    @pl.when(t_idx == 0)
    def _():
        for k in range(2 * nchunk):
            s_scr[k, 0:8, :] = jnp.zeros((8, FF_CHUNK), F32)

    mix = None
    for g, ref in enumerate((oa_ref, ob_ref, oc_ref, od_ref)):
        part = jnp.dot(ref[...].astype(BF16), wo_ref[g * D_GRP:(g + 1) * D_GRP, :], preferred_element_type=F32)
        mix = part if mix is None else mix + part
    x1 = x_ref[...] + mix
    hf = _rms(x1, gffn_ref[...]).astype(BF16)

    def conv_cols(k, c0):
        up = jnp.dot(hf, wup_ref[:, c0:c0 + FF_CHUNK], preferred_element_type=F32)
        s_scr[k, 8:8 + tm, :] = up
        y = (cw_ref[0:1, c0:c0 + FF_CHUNK] * s_scr[k, 6:6 + tm, :]
             + cw_ref[1:2, c0:c0 + FF_CHUNK] * s_scr[k, 7:7 + tm, :]
             + cw_ref[2:3, c0:c0 + FF_CHUNK] * up + cb_ref[:, c0:c0 + FF_CHUNK])
        last2 = s_scr[k, tm + 6:tm + 8, :]
        s_scr[k, 6:8, :] = last2
        tail_ref[:, c0:c0 + FF_CHUNK] = last2
        return y

    acc = None
    for c in range(nchunk):
        gate = conv_cols(2 * c, c * FF_CHUNK)
        lin = conv_cols(2 * c + 1, dff + c * FF_CHUNK)
        act = (gate * jax.nn.sigmoid(gate) * lin).astype(BF16)
        down = jnp.dot(act, wdn_ref[c * FF_CHUNK:(c + 1) * FF_CHUNK, :], preferred_element_type=F32)
        acc = down if acc is None else acc + down
    y_ref[...] = x1 + acc


def _ffn_prompt(x, oa, ob, oc, od, lp, l):
    B, T, D = x.shape
    tm = TOKEN_TILE
    dff = lp["w_down"].shape[1]
    tok = lambda w: pl.BlockSpec((None, tm, w), lambda b, i: (b, i, 0))
    layer = lambda *s: pl.BlockSpec((None,) + s, lambda b, i: (l,) + (0,) * len(s))
    once = lambda *s: pl.BlockSpec((None,) + s, lambda b, i: (l,) + (0,) * len(s), pipeline_mode=pl.Buffered(1))
    return pl.pallas_call(
        _ffn_prompt_kernel,
        grid=(B, T // tm),
        in_specs=[tok(D), tok(D_GRP), tok(D_GRP), tok(D_GRP), tok(D_GRP), once(D, D), layer(1, D),
                  once(D, 2 * dff), layer(8, 2 * dff), layer(1, 2 * dff), once(dff, D)],
        out_specs=[tok(D), pl.BlockSpec((None, 2, 2 * dff), lambda b, i: (b, 0, 0))],
        out_shape=[jax.ShapeDtypeStruct((B, T, D), F32), jax.ShapeDtypeStruct((B, 2, 2 * dff), F32)],
        scratch_shapes=[pltpu.VMEM((2 * dff // FF_CHUNK, tm + 8, FF_CHUNK), F32)],
        compiler_params=pltpu.CompilerParams(dimension_semantics=("arbitrary", "arbitrary"),
                                             vmem_limit_bytes=VMEM_LIMIT),
        name="ffn_prompt",
    )(x, oa, ob, oc, od, lp["w_out"], lp["norm_ffn"], lp["w_up"], lp["ffn_conv_w"], lp["ffn_conv_b"], lp["w_down"])


def _inproj_sample_kernel(x_ref, gmix_ref, w_ref, bd_ref, gains_ref, fb_ref, cos_ref, sin_ref, wmix_ref, bmix_ref,
                          hist_ref, cw_ref, cb_ref, g_ref, beta_ref,
                          oa_ref, od_ref, qb_ref, kb_ref, vb_ref, qc_ref, kc_ref, vc_ref, lf_ref, sv_ref, tail_ref):
    ts = wmix_ref.shape[0]
    ns = x_ref.shape[0] // ts
    h = _rms(x_ref[...], gmix_ref[...]).astype(BF16)
    u, sv, qb, kb, vb, qc, kc, vc, logf, glu = _in_groups(
        h, w_ref, bd_ref[...], gains_ref[...], fb_ref[...], cos_ref[...], sin_ref[...])
    qb_ref[...] = qb
    kb_ref[...] = kb
    vb_ref[...] = vb
    qc_ref[...] = qc
    kc_ref[...] = kc
    vc_ref[...] = vc
    lf_ref[...] = logf
    sv_ref[...] = sv

    for t in range(ts):
        mixed = bmix_ref[t:t + 1, :]
        for s in range(t + 1):
            mixed = mixed + wmix_ref[t, s:s + 1, :] * sv[s * ns:(s + 1) * ns]
        oa_ref[t * ns:(t + 1) * ns, :] = u[t * ns:(t + 1) * ns] * mixed

    nh = hist_ref.shape[0]
    slabs = [hist_ref[r] for r in range(nh)] + [glu[t * ns:(t + 1) * ns] for t in range(ts)]
    for r in range(nh):
        tail_ref[r] = slabs[ts + r]
    for t in range(ts):
        acc = jnp.zeros((ns, D_GRP), F32) + cb_ref[...]
        for tap in range(CONV_W):
            acc = acc + cw_ref[tap:tap + 1, :] * slabs[t + tap]
        od_ref[t * ns:(t + 1) * ns, :] = _layer_norm_silu(acc, g_ref[...], beta_ref[...])


def _inproj_sample(x_tm, hist_tm, lp, consts, l):
    n, D = x_tm.shape
    ts, ns = consts["ts"], n // consts["ts"]
    din = lp["w_in"].shape[-1]
    layer = lambda *s: pl.BlockSpec((None,) + s, lambda i: (l,) + (0,) * len(s))
    const = lambda *s: pl.BlockSpec(s, lambda i: (0,) * len(s))
    tok = jax.ShapeDtypeStruct((n, D_GRP), F32)
    nh = CONV_W - 1
    return pl.pallas_call(
        _inproj_sample_kernel,
        grid=(1,),
        in_specs=[const(n, D), layer(1, D), layer(D, din), const(D_GRP, D_GRP), layer(8, D_GRP), layer(1, LANES),
                  const(n, D_GRP), const(n, D_GRP), layer(ts, ts, D_GRP), layer(ts, D_GRP),
                  layer(nh, ns, D_GRP), layer(32, D_GRP), layer(1, D_GRP), layer(1, D_GRP), layer(1, D_GRP)],
        out_specs=[const(n, D_GRP)] * 8 + [const(n, LANES), const(n, D_GRP), const(nh, ns, D_GRP)],
        out_shape=[tok] * 8 + [jax.ShapeDtypeStruct((n, LANES), F32), tok,
                               jax.ShapeDtypeStruct((nh, ns, D_GRP), F32)],
        compiler_params=pltpu.CompilerParams(dimension_semantics=("arbitrary",), vmem_limit_bytes=VMEM_LIMIT),
        name="inproj_sample",
    )(x_tm, lp["norm_mix"], lp["w_in"], consts["bd"], lp["gains"], lp["fb"], consts["cos_s"], consts["sin_s"],
      lp["sgu_w_s"], lp["sgu_b_s"], hist_tm, lp["conv_w"], lp["conv_b"], lp["conv_ln_g"], lp["conv_ln_b"])


def _attn_sample_kernel(pt_ref, qb_ref, kb_ref, vb_ref, qc_ref, kc_ref, vc_ref, lf_ref, expand_ref,
                        pkb_ref, pvb_ref, pkc_ref, pvc_ref, plf_ref,
                        ob_ref, oc_ref,
                        kb_buf, vb_buf, kc_buf, vc_buf, lf_buf, sem, *, layer, n_pages):
    b = pl.program_id(0)
    nb = pl.num_programs(0)
    ts = qb_ref.shape[0]
    page = pkb_ref.shape[3]
    past = n_pages * page
    pools = (pkb_ref, pvb_ref, pkc_ref, pvc_ref)
    bufs = (kb_buf, vb_buf, kc_buf, vc_buf)

    def copies(seq, slot):
        out = []
        for p in range(n_pages):
            phys = pt_ref[seq * n_pages + p]
            for a in range(4):
                out.append(pltpu.make_async_copy(pools[a].at[layer, phys],
                                                 bufs[a].at[slot, :, pl.ds(p * page, page)], sem.at[slot, a]))
            out.append(pltpu.make_async_copy(plf_ref.at[layer, phys],
                                             lf_buf.at[slot, pl.ds(p * N_HEADS, N_HEADS), :], sem.at[slot, 4]))
        return out

    slot = b % 2

    @pl.when(b == 0)
    def _():
        for cp in copies(0, 0):
            cp.start()

    @pl.when(b + 1 < nb)
    def _():
        for cp in copies(b + 1, 1 - slot):
            cp.start()

    for cp in copies(b, slot):
        cp.wait()

    rows = 8 * N_HEADS
    head_l = _head_of_lane((8, D_GRP))
    sub8 = lax.broadcasted_iota(jnp.int32, (8, D_GRP), 0)

    def rows8(ref, width=D_GRP):
        sub = lax.broadcasted_iota(jnp.int32, (8, width), 0)
        out = jnp.zeros((8, width), F32)
        for t in range(ts):
            out = jnp.where(sub == t, ref[t, pl.ds(b, 1), :], out)
        return out

    def q_rows(ref):
        q8 = rows8(ref)
        return jnp.concatenate([jnp.where(head_l == h, q8, 0.0) for h in range(N_HEADS)], axis=0)

    r_sub = lax.broadcasted_iota(jnp.int32, (rows, 8), 0) % 8
    c_new = lax.broadcasted_iota(jnp.int32, (rows, 8), 1)
    new_ok = (c_new <= r_sub) & (c_new < ts)
    new_ok = new_ok | ((r_sub >= ts) & (c_new == 0))

    def attend(vT, v_new, s_past, s_new):
        m = jnp.maximum(jnp.max(s_past, axis=1, keepdims=True), jnp.max(s_new, axis=1, keepdims=True))
        p_past = jnp.exp(s_past - m)
        p_new = jnp.exp(s_new - m)
        den = jnp.sum(p_past, axis=1, keepdims=True) + jnp.sum(p_new, axis=1, keepdims=True)
        o = lax.dot_general(p_past.astype(BF16), vT, (((1,), (1,)), ((), ())), preferred_element_type=F32)
        o = o + jnp.dot(p_new.astype(BF16), v_new.astype(BF16), preferred_element_type=F32)
        o = o / den
        out8 = jnp.zeros((8, D_GRP), F32)
        for h in range(N_HEADS):
            out8 = jnp.where(head_l == h, o[8 * h:8 * h + 8], out8)
        return out8

    def write(o_ref, out8):
        for t in range(ts):
            o_ref[t, pl.ds(b, 1), :] = out8[t:t + 1]

    qr = q_rows(qb_ref)
    kTf = kb_buf[slot]
    n_blk = past // MOBA_BLOCK
    lane_k = lax.broadcasted_iota(jnp.int32, (D_GRP, LANES), 1)
    km = jnp.zeros((D_GRP, LANES), F32)
    for nblk in range(n_blk):
        km = jnp.where(lane_k == nblk,
                       jnp.mean(kTf[:, nblk * MOBA_BLOCK:(nblk + 1) * MOBA_BLOCK], axis=1, keepdims=True), km)
    gate = jnp.dot(qr, km, precision=HIGHEST, preferred_element_type=F32)
    lane_g = lax.broadcasted_iota(jnp.int32, (rows, LANES), 1)
    rank = jnp.zeros((rows, LANES), F32)
    for mblk in range(n_blk):
        gm = gate[:, mblk:mblk + 1]
        rank = rank + jnp.where((gm > gate) | ((gm == gate) & (mblk < lane_g)), 1.0, 0.0)
    sel = jnp.where((rank < MOBA_TOPK) & (lane_g < n_blk), 1.0, 0.0)
    qrb = qr.astype(BF16)
    s_full = jnp.dot(qrb, kTf.astype(BF16), preferred_element_type=F32)
    s_past = jnp.concatenate(
        [jnp.where(sel[:, nblk:nblk + 1] > 0.5, s_full[:, nblk * MOBA_BLOCK:(nblk + 1) * MOBA_BLOCK], NEG_INF)
         for nblk in range(n_blk)], axis=1)
    k_new = rows8(kb_ref)
    v_new = rows8(vb_ref)
    s_new = lax.dot_general(qrb, k_new.astype(BF16), (((1,), (1,)), ((), ())), preferred_element_type=F32)
    s_new = jnp.where(new_ok, s_new, NEG_INF)
    write(ob_ref, attend(vb_buf[slot].astype(BF16), v_new, s_past, s_new))

    x = lf_buf[slot]
    r = lax.broadcasted_iota(jnp.int32, (page, page), 0)
    c = lax.broadcasted_iota(jnp.int32, (page, page), 1)
    upper = jnp.where(r <= c, 1.0, 0.0).astype(F32)
    within = jnp.dot(x, upper, precision=HIGHEST, preferred_element_type=F32)
    np4 = n_pages * N_HEADS
    rr = lax.broadcasted_iota(jnp.int32, (np4, np4), 0)
    cc = lax.broadcasted_iota(jnp.int32, (np4, np4), 1)
    same_head = (rr % N_HEADS) == (cc % N_HEADS)
    tot = jnp.broadcast_to(within[:, page - 1:page], (np4, page))
    before = jnp.dot(jnp.where(same_head & (cc < rr - rr % N_HEADS), 1.0, 0.0).astype(F32), tot,
                     precision=HIGHEST, preferred_element_type=F32)
    total = jnp.dot(jnp.where(same_head, 1.0, 0.0).astype(F32), tot, precision=HIGHEST, preferred_element_type=F32)
    after = total - (within + before)
    bias = jnp.dot(expand_ref[...], after, precision=HIGHEST, preferred_element_type=F32)
    lf8 = rows8(lf_ref, LANES)
    cum_rows = []
    run = jnp.zeros((1, LANES), F32)
    for t in range(ts):
        run = run + lf8[t:t + 1]
        cum_rows.append(run)
    sub81 = lax.broadcasted_iota(jnp.int32, (8, 1), 0)
    cq_parts = []
    for h in range(N_HEADS):
        part = jnp.zeros((8, 1), F32)
        for t in range(ts):
            part = jnp.where(sub81 == t, cum_rows[t][:, h:h + 1], part)
        cq_parts.append(part)
    cq = jnp.concatenate(cq_parts, axis=0)
    cnew_cols = jnp.zeros((rows, 8), F32)
    for t in range(ts):
        col_t = jnp.concatenate([jnp.broadcast_to(cum_rows[t][:, h:h + 1], (8, 1)) for h in range(N_HEADS)],
                                axis=0)
        cnew_cols = jnp.where(c_new == t, col_t, cnew_cols)
    qr = q_rows(qc_ref)
    qrb = qr.astype(BF16)
    kTf = kc_buf[slot]
    s_full = jnp.dot(qrb, kTf.astype(BF16), preferred_element_type=F32)
    s_past = jnp.concatenate([s_full[:, p * page:(p + 1) * page] + bias[p * rows:(p + 1) * rows]
                              for p in range(n_pages)], axis=1) + cq
    k_new = rows8(kc_ref)
    v_new = rows8(vc_ref)
    s_new = lax.dot_general(qrb, k_new.astype(BF16), (((1,), (1,)), ((), ())), preferred_element_type=F32)
    s_new = jnp.where(new_ok, s_new + cq - cnew_cols, NEG_INF)
    write(oc_ref, attend(vc_buf[slot].astype(BF16), v_new, s_past, s_new))


def _attn_sample(page_table, qb, kb, vb, qc, kc, vc, lf, pools, consts, l):
    ns, n_pages = page_table.shape
    ts = consts["ts"]
    pkb, pvb, pkc, pvc, plf = pools
    page = pkb.shape[3]
    past = n_pages * page
    tm3 = lambda a: a.reshape(ts, ns, a.shape[-1])
    whole = lambda w: pl.BlockSpec((ts, ns, w), lambda b, pt: (0, 0, 0))
    anyspec = pl.BlockSpec(memory_space=pl.ANY)
    grid_spec = pltpu.PrefetchScalarGridSpec(
        num_scalar_prefetch=1,
        grid=(ns,),
        in_specs=[whole(D_GRP)] * 6 + [whole(LANES), pl.BlockSpec(consts["expand"].shape, lambda b, pt: (0, 0))]
                 + [anyspec] * 5,
        out_specs=[whole(D_GRP), whole(D_GRP)],
        scratch_shapes=[pltpu.VMEM((2, D_GRP, past), F32)] * 4
                       + [pltpu.VMEM((2, n_pages * N_HEADS, page), F32), pltpu.SemaphoreType.DMA((2, 5))],
    )
    ob, oc = pl.pallas_call(
        functools.partial(_attn_sample_kernel, layer=l, n_pages=n_pages),
        grid_spec=grid_spec,
        out_shape=[jax.ShapeDtypeStruct((ts, ns, D_GRP), F32)] * 2,
        compiler_params=pltpu.CompilerParams(dimension_semantics=("arbitrary",), vmem_limit_bytes=VMEM_LIMIT),
        name="attn_sample",
    )(page_table.reshape(-1), tm3(qb), tm3(kb), tm3(vb), tm3(qc), tm3(kc), tm3(vc), tm3(lf), consts["expand"],
      pkb, pvb, pkc, pvc, plf)
    return ob.reshape(ts * ns, D_GRP), oc.reshape(ts * ns, D_GRP)


def _ffn_sample_kernel(x_ref, oa_ref, ob_ref, oc_ref, od_ref, wo_ref, gffn_ref, wup_ref, cw_ref, cb_ref, wdn_ref,
                       fbuf_ref, y_ref, tail_ref, *, ts):
    n = x_ref.shape[0]
    ns = n // ts
    dff = wdn_ref.shape[0]

    mix = None
    for g, ref in enumerate((oa_ref, ob_ref, oc_ref, od_ref)):
        part = jnp.dot(ref[...].astype(BF16), wo_ref[g * D_GRP:(g + 1) * D_GRP, :], preferred_element_type=F32)
        mix = part if mix is None else mix + part
    x1 = x_ref[...] + mix
    hf = _rms(x1, gffn_ref[...]).astype(BF16)

    def conv_cols(c0):
        up = jnp.dot(hf, wup_ref[:, c0:c0 + FF_CHUNK], preferred_element_type=F32)
        slabs = [fbuf_ref[r, :, c0:c0 + FF_CHUNK] for r in range(FFN_CONV_W - 1)]
        slabs += [up[t * ns:(t + 1) * ns] for t in range(ts)]
        for r in range(FFN_CONV_W - 1):
            tail_ref[r, :, c0:c0 + FF_CHUNK] = slabs[ts + r]
        outs = []
        for t in range(ts):
            y = cb_ref[:, c0:c0 + FF_CHUNK]
            for tap in range(FFN_CONV_W):
                y = y + cw_ref[tap:tap + 1, c0:c0 + FF_CHUNK] * slabs[t + tap]
            outs.append(y)
        return jnp.concatenate(outs, axis=0)

    acc = None
    for c in range(dff // FF_CHUNK):
        gate = conv_cols(c * FF_CHUNK)
        lin = conv_cols(dff + c * FF_CHUNK)
        act = (gate * jax.nn.sigmoid(gate) * lin).astype(BF16)
        down = jnp.dot(act, wdn_ref[c * FF_CHUNK:(c + 1) * FF_CHUNK, :], preferred_element_type=F32)
        acc = down if acc is None else acc + down
    y_ref[...] = x1 + acc


def _ffn_sample(x_tm, oa, ob, oc, od, fbuf_tm, lp, consts, l):
    n, D = x_tm.shape
    dff = lp["w_down"].shape[1]
    ns = n // consts["ts"]
    layer = lambda *s: pl.BlockSpec((None,) + s, lambda i: (l,) + (0,) * len(s), pipeline_mode=pl.Buffered(1))
    const = lambda *s: pl.BlockSpec(s, lambda i: (0,) * len(s))
    return pl.pallas_call(
        functools.partial(_ffn_sample_kernel, ts=consts["ts"]),
        grid=(1,),
        in_specs=[const(n, D)] + [const(n, D_GRP)] * 4 + [layer(D, D), layer(1, D), layer(D, 2 * dff),
                  layer(8, 2 * dff), layer(1, 2 * dff), layer(dff, D), const(2, ns, 2 * dff)],
        out_specs=[const(n, D), const(2, ns, 2 * dff)],
        out_shape=[jax.ShapeDtypeStruct((n, D), F32), jax.ShapeDtypeStruct((2, ns, 2 * dff), F32)],
        compiler_params=pltpu.CompilerParams(dimension_semantics=("arbitrary",), vmem_limit_bytes=VMEM_LIMIT),
        name="ffn_sample",
    )(x_tm, oa, ob, oc, od, lp["w_out"], lp["norm_ffn"], lp["w_up"], lp["ffn_conv_w"], lp["ffn_conv_b"],
      lp["w_down"], fbuf_tm)


def _pad_rows(a, rows):
    return jnp.pad(a, ((0, 0), (0, rows - a.shape[1]), (0, 0)))


def _rope_tables(pos):
    half = HEAD_DIM // 2
    inv = ROPE_THETA ** (-jnp.arange(half, dtype=F32) / half)
    ang = pos.astype(F32)[:, None] * inv[None, :]
    cos = jnp.tile(jnp.cos(ang), (1, 2 * N_HEADS))
    sin = jnp.sin(ang)
    sin_signed = jnp.tile(jnp.concatenate([-sin, sin], axis=1), (1, N_HEADS))
    return cos, sin_signed


def _prepare(norm_mix, w_in, sgu_norm, sgu_w, sgu_b, moba_qn, moba_kn, fox_qn, fox_kn, fox_fb, conv_w, conv_b,
             conv_ln_g, conv_ln_b, w_out, norm_ffn, w_up, ffn_conv_w, ffn_conv_b, w_down, ts):
    depth = w_in.shape[0]
    n_main = 8 * D_GRP
    w_in_p = jnp.concatenate([w_in[:, :, :n_main], w_in[:, :, n_main + N_HEADS:], w_in[:, :, n_main:n_main + N_HEADS],
                              jnp.zeros(w_in.shape[:2] + (LANES - N_HEADS,), w_in.dtype)], axis=-1).astype(BF16)
    per_head = lambda g: jnp.tile(g, (1, N_HEADS))
    gains = jnp.stack([sgu_norm.reshape(depth, D_GRP), per_head(moba_qn), per_head(moba_kn), per_head(fox_qn),
                       per_head(fox_kn)], axis=1)
    lanes_of_head = lambda a: jnp.repeat(a, HEAD_DIM, axis=-1)
    return {
        "norm_mix": norm_mix[:, None, :],
        "w_in": w_in_p,
        "gains": _pad_rows(gains, 8),
        "fb": jnp.pad(fox_fb, ((0, 0), (0, LANES - N_HEADS)))[:, None, :],
        "sgu_w": sgu_w,
        "sgu_bmix": lanes_of_head(jnp.swapaxes(sgu_b, 1, 2)),
        "sgu_w_s": lanes_of_head(jnp.transpose(sgu_w[:, :, :ts, :ts], (0, 2, 3, 1))),
        "sgu_b_s": lanes_of_head(jnp.swapaxes(sgu_b[:, :, :ts], 1, 2)),
        "conv_w": _pad_rows(conv_w, 32),
        "conv_b": conv_b[:, None, :],
        "conv_ln_g": conv_ln_g[:, None, :],
        "conv_ln_b": conv_ln_b[:, None, :],
        "w_out": w_out.astype(BF16),
        "norm_ffn": norm_ffn[:, None, :],
        "w_up": w_up.astype(BF16),
        "ffn_conv_w": _pad_rows(ffn_conv_w, 8),
        "ffn_conv_b": ffn_conv_b[:, None, :],
        "w_down": w_down.astype(BF16),
    }


def _constants(T, ts, ns, past_len, n_pages):
    blk = np.arange(D_GRP) // HEAD_DIM
    bd = jnp.asarray((blk[:, None] == blk[None, :]).astype(np.float32) / HEAD_DIM, dtype=BF16)
    cos_p, sin_p = _rope_tables(jnp.arange(T, dtype=jnp.int32))
    pos_s = past_len + jnp.repeat(jnp.arange(ts, dtype=jnp.int32), ns)
    cos_s, sin_s = _rope_tables(pos_s)
    rows = np.arange(n_pages * 8 * N_HEADS)
    src = (rows // (8 * N_HEADS)) * N_HEADS + (rows % (8 * N_HEADS)) // 8
    expand = jnp.asarray((src[:, None] == np.arange(n_pages * N_HEADS)[None, :]).astype(np.float32))
    return {"bd": bd, "cos_p": cos_p, "sin_p": sin_p, "cos_s": cos_s, "sin_s": sin_s, "expand": expand, "ts": ts}


def kernel(x_prompt, x_sample, cache_moba_k, cache_moba_v, cache_fox_k, cache_fox_v, cache_fox_logf, state_conv, state_ffn_conv, page_table, norm_mix, w_in, sgu_norm, sgu_w, sgu_b, moba_qn, moba_kn, fox_qn, fox_kn, fox_fb, conv_w, conv_b, conv_ln_g, conv_ln_b, w_out, norm_ffn, w_up, ffn_conv_w, ffn_conv_b, w_down):
    B, T, D = x_prompt.shape
    ns, ts, _ = x_sample.shape
    depth = w_in.shape[0]
    n_pages = page_table.shape[1]
    page = cache_moba_k.shape[2]
    past_len = n_pages * page
    assert T % ATT_BLOCK == 0 and T % TOKEN_TILE == 0 and past_len % MOBA_BLOCK == 0 and ts <= 8

    lp = _prepare(norm_mix, w_in, sgu_norm, sgu_w, sgu_b, moba_qn, moba_kn, fox_qn, fox_kn, fox_fb, conv_w, conv_b,
                  conv_ln_g, conv_ln_b, w_out, norm_ffn, w_up, ffn_conv_w, ffn_conv_b, w_down, ts)
    consts = _constants(T, ts, ns, past_len, n_pages)

    chan_major = lambda c: jnp.transpose(c, (0, 1, 3, 4, 2)).reshape(c.shape[0], c.shape[1], D_GRP, c.shape[2])
    pools = (chan_major(cache_moba_k), chan_major(cache_moba_v), chan_major(cache_fox_k), chan_major(cache_fox_v),
             jnp.transpose(cache_fox_logf, (0, 1, 3, 2)))
    hist_tm = jnp.transpose(state_conv, (0, 2, 1, 3))
    fbuf_tm = jnp.transpose(state_ffn_conv, (0, 2, 1, 3))

    xp = x_prompt
    xs = jnp.transpose(x_sample, (1, 0, 2)).reshape(ts * ns, D)
    P = {k: [] for k in ("kb", "vb", "kc", "vc", "lf", "conv", "ffn")}
    S = {k: [] for k in ("kb", "vb", "kc", "vc", "lf", "conv", "ffn", "sv")}
    for l in range(depth):
        oa, qb, kbT, vbT, qc, kcT, vcT, lfT, glu = _inproj_prompt(xp, lp, consts, l)
        ob = _attn_prompt(qb, kbT, vbT, lfT, fox=False)
        oc = _attn_prompt(qc, kcT, vcT, lfT, fox=True)
        od, conv_tail = _conv_prompt(glu, lp, l)
        xp, ffn_tail = _ffn_prompt(xp, oa, ob, oc, od, lp, l)
        for k, v in zip(("kb", "vb", "kc", "vc", "lf", "conv", "ffn"), (kbT, vbT, kcT, vcT, lfT, conv_tail, ffn_tail)):
            P[k].append(v)

        soa, sod, sqb, skb, svb, sqc, skc, svc, slf, ssv, s_tail = _inproj_sample(xs, hist_tm, lp, consts, l)
        sob, soc = _attn_sample(page_table, sqb, skb, svb, sqc, skc, svc, slf, pools, consts, l)
        xs, sffn_tail = _ffn_sample(xs, soa, sob, soc, sod, fbuf_tm[l], lp, consts, l)
        for k, v in zip(("kb", "vb", "kc", "vc", "lf", "conv", "ffn", "sv"),
                        (skb, svb, skc, svc, slf, s_tail, sffn_tail, ssv)):
            S[k].append(v)

    P = {k: jnp.stack(v) for k, v in P.items()}
    S = {k: jnp.stack(v) for k, v in S.items()}
    heads_p = lambda a: jnp.transpose(a.reshape(depth, B, N_HEADS, HEAD_DIM, T), (0, 1, 4, 2, 3))
    heads_s = lambda a: jnp.transpose(a.reshape(depth, ts, ns, N_HEADS, HEAD_DIM), (0, 2, 1, 3, 4))
    return (xp, jnp.transpose(xs.reshape(ts, ns, D), (1, 0, 2)),
            heads_p(P["kb"]), heads_p(P["vb"]), heads_p(P["kc"]), heads_p(P["vc"]),
            jnp.transpose(P["lf"], (0, 1, 3, 2)), P["conv"], P["ffn"],
            heads_s(S["kb"]), heads_s(S["vb"]), heads_s(S["kc"]), heads_s(S["vc"]),
            jnp.transpose(S["lf"].reshape(depth, ts, ns, LANES)[..., :N_HEADS], (0, 2, 1, 3)),
            jnp.transpose(S["conv"], (0, 2, 1, 3)), jnp.transpose(S["ffn"], (0, 2, 1, 3)),
            jnp.transpose(S["sv"].reshape(depth, ts, ns, D_GRP), (0, 2, 1, 3)))
```

```python
import functools

import numpy as np
import jax
import jax.numpy as jnp
from jax import lax
from jax.experimental import pallas as pl
from jax.experimental.pallas import tpu as pltpu

F32 = jnp.float32
BF16 = jnp.bfloat16
HIGHEST = lax.Precision.HIGHEST

HEAD_DIM = 64
N_HEADS = 4
D_GRP = N_HEADS * HEAD_DIM
SGU_CHUNK = 128
MOBA_BLOCK = 256
MOBA_TOPK = 3
CONV_W = 31
FFN_CONV_W = 3
ROPE_THETA = 10000.0
EPS = 1e-6
NEG_INF = -1e30
LANES = 128
Q_SCALE = HEAD_DIM ** -0.5

TOKEN_TILE = 256
FFN_TILE = 512
ATT_BLOCK = 256
FF_CHUNK = 256
CONV_ROWS = 64
VMEM_LIMIT = 52 * 1024 * 1024


def _gelu(x):
    return 0.5 * x * (1.0 + jnp.tanh(0.7978845608028654 * (x + 0.044715 * (x * x * x))))


def _log_sigmoid(x):
    return jnp.minimum(x, 0.0) - jnp.log1p(jnp.exp(-jnp.abs(x)))


def _rms(x, gain):
    ms = jnp.mean(x * x, axis=-1, keepdims=True)
    return x * lax.rsqrt(ms + EPS) * gain


def _head_rms(y, gain, bd):
    sq = y * y
    hi = sq.astype(BF16)
    lo = (sq - hi.astype(F32)).astype(BF16)
    ms = jnp.dot(hi, bd, preferred_element_type=F32) + jnp.dot(lo, bd, preferred_element_type=F32)
    return y * lax.rsqrt(ms + EPS) * gain


def _rope(y, cos, sin_signed):
    lane = lax.broadcasted_iota(jnp.int32, y.shape, 1)
    first = (lane % HEAD_DIM) < (HEAD_DIM // 2)
    partner = jnp.where(first, pltpu.roll(y, D_GRP - HEAD_DIM // 2, 1), pltpu.roll(y, HEAD_DIM // 2, 1))
    return y * cos + partner * sin_signed


def _head_of_lane(shape):
    return lax.broadcasted_iota(jnp.int32, shape, len(shape) - 1) // HEAD_DIM


def _in_groups(h, w_ref, bd, gains, fb, cos, sin_s):
    def grp(g, width=D_GRP):
        return jnp.dot(h, w_ref[:, g * D_GRP:g * D_GRP + width], preferred_element_type=F32)

    u = _gelu(grp(0))
    sv = _head_rms(_gelu(grp(1)), gains[0:1], bd)
    qb = _rope(_head_rms(grp(2), gains[1:2], bd), cos, sin_s) * Q_SCALE
    kb = _rope(_head_rms(grp(3), gains[2:3], bd), cos, sin_s)
    vb = grp(4)
    qc = _head_rms(grp(5), gains[3:4], bd) * Q_SCALE
    kc = _head_rms(grp(6), gains[4:5], bd)
    vc = grp(7)
    glu = grp(8) * jax.nn.sigmoid(grp(9))
    logf = _log_sigmoid(grp(10, LANES) + fb)
    return u, sv, qb, kb, vb, qc, kc, vc, logf, glu


def _layer_norm_silu(y, g, b):
    mu = jnp.mean(y, axis=-1, keepdims=True)
    yc = y - mu
    var = jnp.mean(yc * yc, axis=-1, keepdims=True)
    z = yc * lax.rsqrt(var + EPS) * g + b
    return z * jax.nn.sigmoid(z)


def _inproj_prompt_kernel(x_ref, gmix_ref, w_ref, bd_ref, gains_ref, fb_ref, cos_ref, sin_ref, wmix_ref, bmix_ref,
                          *refs, n_carried):
    (kbT_ref, vbT_ref, kcT_ref, vcT_ref, lfT_ref, oa_ref, glu_ref,
     qbT_ref, qcT16_ref, kb16_ref, kc16_ref, vbT16_ref, vcT16_ref, km_ref) = refs[n_carried:]
    h = _rms(x_ref[...], gmix_ref[...]).astype(BF16)
    u, sv, qb, kb, vb, qc, kc, vc, logf, glu = _in_groups(
        h, w_ref, bd_ref[...], gains_ref[...], fb_ref[...], cos_ref[...], sin_ref[...])
    glu_ref[...] = glu
    qbT_ref[...] = qb.T
    qcT16_ref[...] = qc.T.astype(BF16)
    kb16_ref[...] = kb.astype(BF16)
    kc16_ref[...] = kc.astype(BF16)
    kbT_ref[...] = kb.T
    kcT_ref[...] = kc.T
    for src, full_ref, half_ref in ((vb, vbT_ref, vbT16_ref), (vc, vcT_ref, vcT16_ref)):
        srcT = src.T
        full_ref[...] = srcT
        half_ref[...] = srcT.astype(BF16)
    lfT_ref[...] = logf.T[0:N_HEADS, :]
    for blk in range(km_ref.shape[0]):
        mean = jnp.mean(kb[blk * MOBA_BLOCK:(blk + 1) * MOBA_BLOCK], axis=0, keepdims=True)
        km_ref[blk] = jnp.broadcast_to(mean, (8, D_GRP))

    n = SGU_CHUNK
    row = lax.broadcasted_iota(jnp.int32, (n, n), 0)
    col = lax.broadcasted_iota(jnp.int32, (n, n), 1)
    head = _head_of_lane((n, D_GRP))
    svb = sv.astype(BF16)
    for c in range(x_ref.shape[0] // n):
        svc = svb[c * n:(c + 1) * n]
        mixed = bmix_ref[...]
        for g in range(N_HEADS):
            wg = jnp.where(row >= col, wmix_ref[g], 0.0).astype(BF16)
            mixed = mixed + jnp.where(head == g, jnp.dot(wg, svc, preferred_element_type=F32), 0.0)
        oa_ref[c * n:(c + 1) * n, :] = u[c * n:(c + 1) * n] * mixed


def _inproj_prompt(x, lp, consts, l, carried):
    B, T, D = x.shape
    depth = lp["w_in"].shape[0]
    tm = TOKEN_TILE
    nt = T // tm
    din = lp["w_in"].shape[-1]
    tok = lambda w: pl.BlockSpec((None, tm, w), lambda b, i: (b, i, 0))
    chan = lambda r: pl.BlockSpec((None, r, tm), lambda b, i: (b, 0, i))
    chan_l = lambda r: pl.BlockSpec((None, None, r, tm), lambda b, i: (l, b, 0, i))
    layer = lambda *s: pl.BlockSpec((None,) + s, lambda b, i: (l,) + (0,) * len(s))
    const = lambda *s: pl.BlockSpec(s, lambda b, i: (0,) * len(s))
    carried = tuple(carried)
    n_in = 10
    outs = pl.pallas_call(
        functools.partial(_inproj_prompt_kernel, n_carried=len(carried)),
        grid=(B, nt),
        in_specs=[tok(D), layer(1, D), layer(D, din), const(D_GRP, D_GRP), layer(8, D_GRP), layer(1, LANES),
                  pl.BlockSpec((tm, D_GRP), lambda b, i: (i, 0)), pl.BlockSpec((tm, D_GRP), lambda b, i: (i, 0)),
                  layer(N_HEADS, SGU_CHUNK, SGU_CHUNK), layer(SGU_CHUNK, D_GRP)]
                 + [pl.BlockSpec(memory_space=pl.ANY)] * len(carried),
        out_specs=[chan_l(D_GRP), chan_l(D_GRP), chan_l(D_GRP), chan_l(D_GRP), chan_l(N_HEADS), tok(D_GRP), tok(D_GRP),
                   chan(D_GRP), chan(D_GRP), tok(D_GRP), tok(D_GRP), chan(D_GRP), chan(D_GRP),
                   pl.BlockSpec((None, tm // MOBA_BLOCK, 8, D_GRP), lambda b, i: (b, i, 0, 0))],
        out_shape=[jax.ShapeDtypeStruct((depth, B, D_GRP, T), F32)] * 4
                  + [jax.ShapeDtypeStruct((depth, B, N_HEADS, T), F32),
                     jax.ShapeDtypeStruct((B, T, D_GRP), F32), jax.ShapeDtypeStruct((B, T, D_GRP), F32),
                     jax.ShapeDtypeStruct((B, D_GRP, T), F32), jax.ShapeDtypeStruct((B, D_GRP, T), BF16),
                     jax.ShapeDtypeStruct((B, T, D_GRP), BF16), jax.ShapeDtypeStruct((B, T, D_GRP), BF16),
                     jax.ShapeDtypeStruct((B, D_GRP, T), BF16), jax.ShapeDtypeStruct((B, D_GRP, T), BF16),
                     jax.ShapeDtypeStruct((B, T // MOBA_BLOCK, 8, D_GRP), F32)],
        input_output_aliases={n_in + k: k for k in range(len(carried))},
        compiler_params=pltpu.CompilerParams(dimension_semantics=("arbitrary", "arbitrary"),
                                             vmem_limit_bytes=VMEM_LIMIT),
        name="inproj_prompt",
    )(x, lp["norm_mix"], lp["w_in"], consts["bd"], lp["gains"], lp["fb"], consts["cos_p"], consts["sin_p"],
      lp["sgu_w"], lp["sgu_bmix"], *carried)
    return outs[:5], outs[5:]


def _pair_tables(nq):
    qi, kj = [], []
    for i in range(nq):
        qi.append(i)
        kj.append(i)
        for j in range(i):
            qi.append(i)
            kj.append(j)
    return np.asarray(qi, np.int32), np.asarray(kj, np.int32)


def _attn_prompt_kernel(qi_ref, kj_ref, qbT_ref, qcT_ref, kb_ref, vbT_ref, kc_ref, vcT_ref, km_ref, lfT_ref,
                        ob_ref, oc_ref, m_scr, l_scr, acc_scr, qm_scr, crow_scr, ccol_scr, bias_scr):
    s_idx = pl.program_id(1)
    i = qi_ref[s_idx]
    j = kj_ref[s_idx]
    tq = qbT_ref.shape[1]
    tk = kb_ref.shape[0]
    nblk = km_ref.shape[0]
    diag = j == i
    last = jnp.logical_or(j == i - 1, i == 0)

    @pl.when(s_idx == 0)
    def _():
        crow_scr[...] = jnp.zeros_like(crow_scr)
        r = lax.broadcasted_iota(jnp.int32, (LANES, LANES), 0)
        c = lax.broadcasted_iota(jnp.int32, (LANES, LANES), 1)
        upper = jnp.where(r <= c, 1.0, 0.0).astype(F32)
        carry = jnp.zeros((N_HEADS, 1), F32)
        per_blk = tk // LANES
        for cc in range(lfT_ref.shape[1] // LANES):
            w = jnp.dot(lfT_ref[:, cc * LANES:(cc + 1) * LANES], upper, precision=HIGHEST,
                        preferred_element_type=F32) + carry
            crow_scr[cc // per_blk, 0:N_HEADS, (cc % per_blk) * LANES:(cc % per_blk + 1) * LANES] = w
            carry = w[:, LANES - 1:LANES]
        r = lax.broadcasted_iota(jnp.int32, (tk, tk), 0)
        c = lax.broadcasted_iota(jnp.int32, (tk, tk), 1)
        eye = jnp.where(r == c, 1.0, 0.0).astype(F32)
        for blk in range(nblk):
            cols = lax.dot_general(eye, crow_scr[blk], (((1,), (1,)), ((), ())), precision=HIGHEST,
                                   preferred_element_type=F32)
            for h in range(N_HEADS):
                ccol_scr[blk, h] = jnp.broadcast_to(cols[:, h:h + 1], (tk, LANES))

    def attend(a, k_ref, vT_ref, mask):
        s = jnp.dot(k_ref[...], qm_scr[a], preferred_element_type=F32)
        s = mask(s, a)
        m_old = m_scr[a]
        m_new = jnp.maximum(m_old, jnp.max(s, axis=0, keepdims=True))
        alpha = jnp.exp(m_old - m_new)
        p = jnp.exp(s - m_new)
        l_scr[a] = alpha * l_scr[a] + jnp.sum(p, axis=0, keepdims=True)
        m_scr[a] = m_new
        p16 = p.astype(BF16)
        for h in range(N_HEADS):
            slot = N_HEADS * a + h
            pv = jnp.dot(vT_ref[h * HEAD_DIM:(h + 1) * HEAD_DIM, :], p16[:, h * tq:(h + 1) * tq],
                         preferred_element_type=F32)
            acc_scr[slot] = acc_scr[slot] * alpha[:, h * tq:(h + 1) * tq] + pv

    def both(mask):
        attend(0, kb_ref, vbT_ref, mask)
        attend(1, kc_ref, vcT_ref, mask)

    def minus_c(s):
        return s - jnp.concatenate([ccol_scr[j, h] for h in range(N_HEADS) for _ in range(tq // LANES)], axis=1)

    @pl.when(diag)
    def _():
        m_scr[...] = jnp.full_like(m_scr, NEG_INF)
        l_scr[...] = jnp.zeros_like(l_scr)
        acc_scr[...] = jnp.zeros_like(acc_scr)
        qbT = qbT_ref[...]
        qbT16 = qbT.astype(BF16)
        qcT16 = qcT_ref[...]
        head_r = lax.broadcasted_iota(jnp.int32, (D_GRP, tq), 0) // HEAD_DIM
        for h in range(N_HEADS):
            qm_scr[0, :, h * tq:(h + 1) * tq] = jnp.where(head_r == h, qbT16, jnp.zeros_like(qbT16))
            qm_scr[1, :, h * tq:(h + 1) * tq] = jnp.where(head_r == h, qcT16, jnp.zeros_like(qcT16))

        sub = lax.broadcasted_iota(jnp.int32, (8, D_GRP), 0)
        km = jnp.zeros((8, D_GRP), F32)
        for n in range(nblk):
            km = jnp.where(sub == n, km_ref[n], km)
        head_l = _head_of_lane((8, D_GRP))
        blk = lax.broadcasted_iota(jnp.int32, (8, tq), 0)
        for h in range(N_HEADS):
            gate = jnp.dot(jnp.where(head_l == h, km, 0.0), qbT, precision=HIGHEST,
                           preferred_element_type=F32)
            rank = jnp.zeros((8, tq), F32)
            for mblk in range(nblk - 1):
                gm = gate[mblk:mblk + 1, :]
                ahead = jnp.where(blk > mblk, jnp.where(gm >= gate, 1.0, 0.0), jnp.where(gm > gate, 1.0, 0.0))
                rank = rank + ahead * jnp.where(mblk < i, 1.0, 0.0)
            bias_scr[:, h * tq:(h + 1) * tq] = jnp.where((rank < MOBA_TOPK) & (blk < i), 0.0, NEG_INF)

        key = lax.broadcasted_iota(jnp.int32, (tk, N_HEADS * tq), 0)
        qry = lax.broadcasted_iota(jnp.int32, (tk, N_HEADS * tq), 1) % tq
        causal = key <= qry

        def mask(s, a):
            if a == 1:
                s = minus_c(s)
            return jnp.where(causal, s, NEG_INF)

        both(mask)

    @pl.when(jnp.logical_not(diag))
    def _():
        def mask(s, a):
            if a == 1:
                return minus_c(s)
            return s + bias_scr[pl.ds(j, 1), :]

        both(mask)

    @pl.when(last)
    def _():
        for a, o_ref in enumerate((ob_ref, oc_ref)):
            den = l_scr[a]
            oT = jnp.concatenate([acc_scr[N_HEADS * a + h] / den[:, h * tq:(h + 1) * tq] for h in range(N_HEADS)],
                                 axis=0)
            o_ref[...] = oT.T


def _attn_prompt(qbT, qcT16, kb16, vbT16, kc16, vcT16, km, lfT_all, l):
    B, _, T = qbT.shape
    tq = ATT_BLOCK
    nq = T // tq
    assert nq <= 8
    qi, kj = _pair_tables(nq)
    qspec = pl.BlockSpec((None, D_GRP, tq), lambda b, s, qi, kj: (b, 0, qi[s]))
    kspec = pl.BlockSpec((None, tq, D_GRP), lambda b, s, qi, kj: (b, kj[s], 0))
    vspec = pl.BlockSpec((None, D_GRP, tq), lambda b, s, qi, kj: (b, 0, kj[s]))
    grid_spec = pltpu.PrefetchScalarGridSpec(
        num_scalar_prefetch=2,
        grid=(B, len(qi)),
        in_specs=[qspec, qspec, kspec, vspec, kspec, vspec,
                  pl.BlockSpec((None, nq, 8, D_GRP), lambda b, s, qi, kj: (b, 0, 0, 0)),
                  pl.BlockSpec((None, None, N_HEADS, T), lambda b, s, qi, kj: (l, b, 0, 0))],
        out_specs=[pl.BlockSpec((None, tq, D_GRP), lambda b, s, qi, kj: (b, qi[s], 0))] * 2,
        scratch_shapes=[pltpu.VMEM((2, 1, N_HEADS * tq), F32), pltpu.VMEM((2, 1, N_HEADS * tq), F32),
                        pltpu.VMEM((2 * N_HEADS, HEAD_DIM, tq), F32), pltpu.VMEM((2, D_GRP, N_HEADS * tq), BF16),
                        pltpu.VMEM((nq, 8, tq), F32), pltpu.VMEM((nq, N_HEADS, tq, LANES), F32),
                        pltpu.VMEM((8, N_HEADS * tq), F32)],
    )
    return pl.pallas_call(
        _attn_prompt_kernel,
        grid_spec=grid_spec,
        out_shape=[jax.ShapeDtypeStruct((B, T, D_GRP), F32)] * 2,
        compiler_params=pltpu.CompilerParams(dimension_semantics=("arbitrary", "arbitrary"),
                                             vmem_limit_bytes=VMEM_LIMIT),
        name="attn_prompt",
    )(jnp.asarray(qi), jnp.asarray(kj), qbT, qcT16, kb16, vbT16, kc16, vcT16, km, lfT_all)


def _conv_prompt_kernel(x_ref, w_ref, b_ref, g_ref, beta_ref, o_ref, tail_ref, s_scr):
    T = x_ref.shape[0]
    pad = 32
    base = pad - (CONV_W - 1)
    s_scr[0, 0:pad, :] = jnp.zeros((pad, D_GRP), F32)
    s_scr[0, pad:pad + T, :] = x_ref[...]
    tail_ref[...] = s_scr[0, pad + T - (CONV_W - 1):pad + T, :]
    n_rows = T + pad - 8
    piece = n_rows // 7
    assert piece * 7 == n_rows and piece % 8 == 0
    for r in range(1, 8):
        for a in range(0, n_rows, piece):
            s_scr[r, a:a + piece, :] = s_scr[0, a + r:a + r + piece, :]
    w = w_ref[...]
    R = CONV_ROWS

    def chunk(c, _):
        r0 = pl.multiple_of(c * R, R)
        acc = jnp.zeros((R, D_GRP), F32) + b_ref[...]
        for tap in range(CONV_W):
            off = base + tap
            acc = acc + w[tap:tap + 1, :] * s_scr[off % 8, pl.ds(pl.multiple_of(r0 + off - off % 8, 8), R), :]
        o_ref[pl.ds(r0, R), :] = _layer_norm_silu(acc, g_ref[...], beta_ref[...])
        return 0

    lax.fori_loop(0, T // R, chunk, 0)


def _conv_prompt(glu, lp, l):
    B, T, _ = glu.shape
    layer = lambda *s: pl.BlockSpec((None,) + s, lambda b: (l,) + (0,) * len(s))
    return pl.pallas_call(
        _conv_prompt_kernel,
        grid=(B,),
        in_specs=[pl.BlockSpec((None, T, D_GRP), lambda b: (b, 0, 0)), layer(32, D_GRP), layer(1, D_GRP),
                  layer(1, D_GRP), layer(1, D_GRP)],
        out_specs=[pl.BlockSpec((None, T, D_GRP), lambda b: (b, 0, 0)),
                   pl.BlockSpec((None, CONV_W - 1, D_GRP), lambda b: (b, 0, 0))],
        out_shape=[jax.ShapeDtypeStruct((B, T, D_GRP), F32), jax.ShapeDtypeStruct((B, CONV_W - 1, D_GRP), F32)],
        scratch_shapes=[pltpu.VMEM((8, T + 32, D_GRP), F32)],
        compiler_params=pltpu.CompilerParams(dimension_semantics=("arbitrary",), vmem_limit_bytes=VMEM_LIMIT),
        name="conv_prompt",
    )(glu, lp["conv_w"], lp["conv_b"], lp["conv_ln_g"], lp["conv_ln_b"])


def _ffn_prompt_kernel(x_ref, oa_ref, ob_ref, oc_ref, od_ref, wo_ref, gffn_ref, wup_ref, cw_ref, cb_ref, wdn_ref,
                       y_ref, tail_ref, s_scr):
    t_idx = pl.program_id(1)
    tm = x_ref.shape[0]
    dff = wdn_ref.shape[0]
    nchunk = dff // FF_CHUNK

    @pl.when(t_idx == 0)
    def _():
        for k in range(2 * nchunk):
            s_scr[k, 0:8, :] = jnp.zeros((8, FF_CHUNK), F32)

    mix = None
    for g, ref in enumerate((oa_ref, ob_ref, oc_ref, od_ref)):
        part = jnp.dot(ref[...].astype(BF16), wo_ref[g * D_GRP:(g + 1) * D_GRP, :], preferred_element_type=F32)
        mix = part if mix is None else mix + part
    x1 = x_ref[...] + mix
    hf = _rms(x1, gffn_ref[...]).astype(BF16)

    def conv_cols(k, c0):
        up = jnp.dot(hf, wup_ref[:, c0:c0 + FF_CHUNK], preferred_element_type=F32)
        s_scr[k, 8:8 + tm, :] = up
        y = (cw_ref[0:1, c0:c0 + FF_CHUNK] * s_scr[k, 6:6 + tm, :]
             + cw_ref[1:2, c0:c0 + FF_CHUNK] * s_scr[k, 7:7 + tm, :]
             + cw_ref[2:3, c0:c0 + FF_CHUNK] * up + cb_ref[:, c0:c0 + FF_CHUNK])
        last2 = s_scr[k, tm + 6:tm + 8, :]
        s_scr[k, 6:8, :] = last2
        tail_ref[:, c0:c0 + FF_CHUNK] = last2
        return y

    acc = None
    for c in range(nchunk):
        gate = conv_cols(2 * c, c * FF_CHUNK)
        lin = conv_cols(2 * c + 1, dff + c * FF_CHUNK)
        act = (gate * jax.nn.sigmoid(gate) * lin).astype(BF16)
        down = jnp.dot(act, wdn_ref[c * FF_CHUNK:(c + 1) * FF_CHUNK, :], preferred_element_type=F32)
        acc = down if acc is None else acc + down
    y_ref[...] = x1 + acc


def _ffn_prompt(x, oa, ob, oc, od, lp, l):
    B, T, D = x.shape
    tm = FFN_TILE
    dff = lp["w_down"].shape[1]
    tok = lambda w: pl.BlockSpec((None, tm, w), lambda b, i: (b, i, 0))
    layer = lambda *s: pl.BlockSpec((None,) + s, lambda b, i: (l,) + (0,) * len(s))
    once =lambda *s: pl.BlockSpec((None,) + s, lambda b, i: (l,) + (0,) * len(s), pipeline_mode=pl.Buffered(1))
    return pl.pallas_call(
        _ffn_prompt_kernel,
        grid=(B, T // tm),
        in_specs=[tok(D), tok(D_GRP), tok(D_GRP), tok(D_GRP), tok(D_GRP), once(D, D), layer(1, D),
                  once(D, 2 * dff), layer(8, 2 * dff), layer(1, 2 * dff), once(dff, D)],
        out_specs=[tok(D), pl.BlockSpec((None, 2, 2 * dff), lambda b, i: (b, 0, 0))],
        out_shape=[jax.ShapeDtypeStruct((B, T, D), F32), jax.ShapeDtypeStruct((B, 2, 2 * dff), F32)],
        scratch_shapes=[pltpu.VMEM((2 * dff // FF_CHUNK, tm + 8, FF_CHUNK), F32)],
        compiler_params=pltpu.CompilerParams(dimension_semantics=("arbitrary", "arbitrary"),
                                             vmem_limit_bytes=VMEM_LIMIT),
        name="ffn_prompt",
    )(x, oa, ob, oc, od, lp["w_out"], lp["norm_ffn"], lp["w_up"], lp["ffn_conv_w"], lp["ffn_conv_b"], lp["w_down"])


def _inproj_sample_kernel(x_ref, gmix_ref, w_ref, bd_ref, gains_ref, fb_ref, cos_ref, sin_ref, wmix_ref, bmix_ref,
                          hist_ref, cw_ref, cb_ref, g_ref, beta_ref,
                          oa_ref, od_ref, qb_ref, kb_ref, vb_ref, qc_ref, kc_ref, vc_ref, lf_ref, sv_ref, tail_ref):
    ts = wmix_ref.shape[0]
    ns = x_ref.shape[0] // ts
    h = _rms(x_ref[...], gmix_ref[...]).astype(BF16)
    u, sv, qb, kb, vb, qc, kc, vc, logf, glu = _in_groups(
        h, w_ref, bd_ref[...], gains_ref[...], fb_ref[...], cos_ref[...], sin_ref[...])
    qb_ref[...] = qb
    kb_ref[...] = kb
    vb_ref[...] = vb
    qc_ref[...] = qc
    kc_ref[...] = kc
    vc_ref[...] = vc
    lf_ref[...] = logf
    sv_ref[...] = sv

    for t in range(ts):
        mixed = bmix_ref[t:t + 1, :]
        for s in range(t + 1):
            mixed = mixed + wmix_ref[t, s:s + 1, :] * sv[s * ns:(s + 1) * ns]
        oa_ref[t * ns:(t + 1) * ns, :] = u[t * ns:(t + 1) * ns] * mixed

    nh = hist_ref.shape[0]
    slabs = [hist_ref[r] for r in range(nh)] + [glu[t * ns:(t + 1) * ns] for t in range(ts)]
    for r in range(nh):
        tail_ref[r] = slabs[ts + r]
    for t in range(ts):
        acc = jnp.zeros((ns, D_GRP), F32) + cb_ref[...]
        for tap in range(CONV_W):
            acc = acc + cw_ref[tap:tap + 1, :] * slabs[t + tap]
        od_ref[t * ns:(t + 1) * ns, :] = _layer_norm_silu(acc, g_ref[...], beta_ref[...])


def _inproj_sample(x_tm, hist_tm, lp, consts, l):
    n, D = x_tm.shape
    ts, ns = consts["ts"], n // consts["ts"]
    din = lp["w_in"].shape[-1]
    layer = lambda *s: pl.BlockSpec((None,) + s, lambda i: (l,) + (0,) * len(s))
    const = lambda *s: pl.BlockSpec(s, lambda i: (0,) * len(s))
    tok = jax.ShapeDtypeStruct((n, D_GRP), F32)
    nh = CONV_W - 1
    return pl.pallas_call(
        _inproj_sample_kernel,
        grid=(1,),
        in_specs=[const(n, D), layer(1, D), layer(D, din), const(D_GRP, D_GRP), layer(8, D_GRP), layer(1, LANES),
                  const(n, D_GRP), const(n, D_GRP), layer(ts, ts, D_GRP), layer(ts, D_GRP),
                  layer(nh, ns, D_GRP), layer(32, D_GRP), layer(1, D_GRP), layer(1, D_GRP), layer(1, D_GRP)],
        out_specs=[const(n, D_GRP)] * 8 + [const(n, LANES), const(n, D_GRP), const(nh, ns, D_GRP)],
        out_shape=[tok] * 8 + [jax.ShapeDtypeStruct((n, LANES), F32), tok,
                               jax.ShapeDtypeStruct((nh, ns, D_GRP), F32)],
        compiler_params=pltpu.CompilerParams(dimension_semantics=("arbitrary",), vmem_limit_bytes=VMEM_LIMIT),
        name="inproj_sample",
    )(x_tm, lp["norm_mix"], lp["w_in"], consts["bd"], lp["gains"], lp["fb"], consts["cos_s"], consts["sin_s"],
      lp["sgu_w_s"], lp["sgu_b_s"], hist_tm, lp["conv_w"], lp["conv_b"], lp["conv_ln_g"], lp["conv_ln_b"])


def _attn_sample_kernel(pt_ref, qb_ref, kb_ref, vb_ref, qc_ref, kc_ref, vc_ref, lf_ref, expand_ref,
                        pkb_ref, pvb_ref, pkc_ref, pvc_ref, plf_ref,
                        ob_ref, oc_ref,
                        kb_buf, vb_buf, kc_buf, vc_buf, lf_buf, sem, *, layer, n_pages):
    b = pl.program_id(0)
    nb = pl.num_programs(0)
    ts = qb_ref.shape[0]
    page = pkb_ref.shape[3]
    past = n_pages * page
    pools = (pkb_ref, pvb_ref, pkc_ref, pvc_ref)
    bufs = (kb_buf, vb_buf, kc_buf, vc_buf)

    def copies(seq, slot):
        out = []
        for p in range(n_pages):
            phys = pt_ref[seq * n_pages + p]
            for a in range(4):
                out.append(pltpu.make_async_copy(pools[a].at[layer, phys],
                                                 bufs[a].at[slot, :, pl.ds(p * page, page)], sem.at[slot, a]))
            out.append(pltpu.make_async_copy(plf_ref.at[layer, phys],
                                             lf_buf.at[slot, pl.ds(p * N_HEADS, N_HEADS), :], sem.at[slot, 4]))
        return out

    slot = b % 2

    @pl.when(b == 0)
    def _():
        for cp in copies(0, 0):
            cp.start()

    @pl.when(b + 1 < nb)
    def _():
        for cp in copies(b + 1, 1 - slot):
            cp.start()

    for cp in copies(b, slot):
        cp.wait()

    rows = 8 * N_HEADS
    head_l = _head_of_lane((8, D_GRP))
    sub8 = lax.broadcasted_iota(jnp.int32, (8, D_GRP), 0)

    def rows8(ref, width=D_GRP):
        sub = lax.broadcasted_iota(jnp.int32, (8, width), 0)
        out = jnp.zeros((8, width), F32)
        for t in range(ts):
            out = jnp.where(sub == t, ref[t, pl.ds(b, 1), :], out)
        return out

    def q_rows(ref):
        q8 = rows8(ref)
        return jnp.concatenate([jnp.where(head_l == h, q8, 0.0) for h in range(N_HEADS)], axis=0)

    r_sub = lax.broadcasted_iota(jnp.int32, (rows, 8), 0) % 8
    c_new = lax.broadcasted_iota(jnp.int32, (rows, 8), 1)
    new_ok = (c_new <= r_sub) & (c_new < ts)
    new_ok = new_ok | ((r_sub >= ts) & (c_new == 0))

    def attend(vT, v_new, s_past, s_new):
        m = jnp.maximum(jnp.max(s_past, axis=1, keepdims=True), jnp.max(s_new, axis=1, keepdims=True))
        p_past = jnp.exp(s_past - m)
        p_new = jnp.exp(s_new - m)
        den = jnp.sum(p_past, axis=1, keepdims=True) + jnp.sum(p_new, axis=1, keepdims=True)
        o = lax.dot_general(p_past.astype(BF16), vT, (((1,), (1,)), ((), ())), preferred_element_type=F32)
        o = o + jnp.dot(p_new.astype(BF16), v_new.astype(BF16), preferred_element_type=F32)
        o = o / den
        out8 = jnp.zeros((8, D_GRP), F32)
        for h in range(N_HEADS):
            out8 = jnp.where(head_l == h, o[8 * h:8 * h + 8], out8)
        return out8

    def write(o_ref, out8):
        for t in range(ts):
            o_ref[t, pl.ds(b, 1), :] = out8[t:t + 1]

    qr = q_rows(qb_ref)
    kTf = kb_buf[slot]
    n_blk = past // MOBA_BLOCK
    lane_k = lax.broadcasted_iota(jnp.int32, (D_GRP, LANES), 1)
    km = jnp.zeros((D_GRP, LANES), F32)
    for nblk in range(n_blk):
        km = jnp.where(lane_k == nblk,
                       jnp.mean(kTf[:, nblk * MOBA_BLOCK:(nblk + 1) * MOBA_BLOCK], axis=1, keepdims=True), km)
    gate = jnp.dot(qr, km, precision=HIGHEST, preferred_element_type=F32)
    lane_g = lax.broadcasted_iota(jnp.int32, (rows, LANES), 1)
    rank = jnp.zeros((rows, LANES), F32)
    for mblk in range(n_blk):
        gm = gate[:, mblk:mblk + 1]
        rank = rank + jnp.where((gm > gate) | ((gm == gate) & (mblk < lane_g)), 1.0, 0.0)
    sel = jnp.where((rank < MOBA_TOPK) & (lane_g < n_blk), 1.0, 0.0)
    qrb = qr.astype(BF16)
    s_full = jnp.dot(qrb, kTf.astype(BF16), preferred_element_type=F32)
    s_past = jnp.concatenate(
        [jnp.where(sel[:, nblk:nblk + 1] > 0.5, s_full[:, nblk * MOBA_BLOCK:(nblk + 1) * MOBA_BLOCK], NEG_INF)
         for nblk in range(n_blk)], axis=1)
    k_new = rows8(kb_ref)
    v_new = rows8(vb_ref)
    s_new = lax.dot_general(qrb, k_new.astype(BF16), (((1,), (1,)), ((), ())), preferred_element_type=F32)
    s_new = jnp.where(new_ok, s_new, NEG_INF)
    write(ob_ref, attend(vb_buf[slot].astype(BF16), v_new, s_past, s_new))

    x = lf_buf[slot]
    r = lax.broadcasted_iota(jnp.int32, (page, page), 0)
    c = lax.broadcasted_iota(jnp.int32, (page, page), 1)
    upper = jnp.where(r <= c, 1.0, 0.0).astype(F32)
    within = jnp.dot(x, upper, precision=HIGHEST, preferred_element_type=F32)
    np4 = n_pages * N_HEADS
    rr = lax.broadcasted_iota(jnp.int32, (np4, np4), 0)
    cc = lax.broadcasted_iota(jnp.int32, (np4, np4), 1)
    same_head = (rr % N_HEADS) == (cc % N_HEADS)
    tot = jnp.broadcast_to(within[:, page - 1:page], (np4, page))
    before = jnp.dot(jnp.where(same_head & (cc < rr - rr % N_HEADS), 1.0, 0.0).astype(F32), tot,
                     precision=HIGHEST, preferred_element_type=F32)
    total = jnp.dot(jnp.where(same_head, 1.0, 0.0).astype(F32), tot, precision=HIGHEST, preferred_element_type=F32)
    after = total - (within + before)
    bias = jnp.dot(expand_ref[...], after, precision=HIGHEST, preferred_element_type=F32)
    lf8 = rows8(lf_ref, LANES)
    cum_rows = []
    run = jnp.zeros((1, LANES), F32)
    for t in range(ts):
        run = run + lf8[t:t + 1]
        cum_rows.append(run)
    sub81 = lax.broadcasted_iota(jnp.int32, (8, 1), 0)
    cq_parts = []
    for h in range(N_HEADS):
        part = jnp.zeros((8, 1), F32)
        for t in range(ts):
            part = jnp.where(sub81 == t, cum_rows[t][:, h:h + 1], part)
        cq_parts.append(part)
    cq = jnp.concatenate(cq_parts, axis=0)
    cnew_cols = jnp.zeros((rows, 8), F32)
    for t in range(ts):
        col_t = jnp.concatenate([jnp.broadcast_to(cum_rows[t][:, h:h + 1], (8, 1)) for h in range(N_HEADS)],
                                axis=0)
        cnew_cols = jnp.where(c_new == t, col_t, cnew_cols)
    qr = q_rows(qc_ref)
    qrb = qr.astype(BF16)
    kTf = kc_buf[slot]
    s_full = jnp.dot(qrb, kTf.astype(BF16), preferred_element_type=F32)
    s_past = jnp.concatenate([s_full[:, p * page:(p + 1) * page] + bias[p * rows:(p + 1) * rows]
                              for p in range(n_pages)], axis=1) + cq
    k_new = rows8(kc_ref)
    v_new = rows8(vc_ref)
    s_new = lax.dot_general(qrb, k_new.astype(BF16), (((1,), (1,)), ((), ())), preferred_element_type=F32)
    s_new = jnp.where(new_ok, s_new + cq - cnew_cols, NEG_INF)
    write(oc_ref, attend(vc_buf[slot].astype(BF16), v_new, s_past, s_new))


def _attn_sample(page_table, qb, kb, vb, qc, kc, vc, lf, pools, consts, l):
    ns, n_pages = page_table.shape
    ts = consts["ts"]
    pkb, pvb, pkc, pvc, plf = pools
    page = pkb.shape[3]
    past = n_pages * page
    tm3 = lambda a: a.reshape(ts, ns, a.shape[-1])
    whole = lambda w: pl.BlockSpec((ts, ns, w), lambda b, pt: (0, 0, 0))
    anyspec = pl.BlockSpec(memory_space=pl.ANY)
    grid_spec = pltpu.PrefetchScalarGridSpec(
        num_scalar_prefetch=1,
        grid=(ns,),
        in_specs=[whole(D_GRP)] * 6 + [whole(LANES), pl.BlockSpec(consts["expand"].shape, lambda b, pt: (0, 0))]
                 + [anyspec] * 5,
        out_specs=[whole(D_GRP), whole(D_GRP)],
        scratch_shapes=[pltpu.VMEM((2, D_GRP, past), F32)] * 4
                       + [pltpu.VMEM((2, n_pages * N_HEADS, page), F32), pltpu.SemaphoreType.DMA((2, 5))],
    )
    ob, oc = pl.pallas_call(
        functools.partial(_attn_sample_kernel, layer=l, n_pages=n_pages),
        grid_spec=grid_spec,
        out_shape=[jax.ShapeDtypeStruct((ts, ns, D_GRP), F32)] * 2,
        compiler_params=pltpu.CompilerParams(dimension_semantics=("arbitrary",), vmem_limit_bytes=VMEM_LIMIT),
        name="attn_sample",
    )(page_table.reshape(-1), tm3(qb), tm3(kb), tm3(vb), tm3(qc), tm3(kc), tm3(vc), tm3(lf), consts["expand"],
      pkb, pvb, pkc, pvc, plf)
    return ob.reshape(ts * ns, D_GRP), oc.reshape(ts * ns, D_GRP)


def _ffn_sample_kernel(x_ref, oa_ref, ob_ref, oc_ref, od_ref, wo_ref, gffn_ref, wup_ref, cw_ref, cb_ref, wdn_ref,
                       fbuf_ref, y_ref, tail_ref, *, ts):
    n = x_ref.shape[0]
    ns = n // ts
    dff = wdn_ref.shape[0]

    mix = None
    for g, ref in enumerate((oa_ref, ob_ref, oc_ref, od_ref)):
        part = jnp.dot(ref[...].astype(BF16), wo_ref[g * D_GRP:(g + 1) * D_GRP, :], preferred_element_type=F32)
        mix = part if mix is None else mix + part
    x1 = x_ref[...] + mix
    hf = _rms(x1, gffn_ref[...]).astype(BF16)

    def conv_cols(c0):
        up = jnp.dot(hf, wup_ref[:, c0:c0 + FF_CHUNK], preferred_element_type=F32)
        slabs = [fbuf_ref[r, :, c0:c0 + FF_CHUNK] for r in range(FFN_CONV_W - 1)]
        slabs += [up[t * ns:(t + 1) * ns] for t in range(ts)]
        for r in range(FFN_CONV_W - 1):
            tail_ref[r, :, c0:c0 + FF_CHUNK] = slabs[ts + r]
        outs = []
        for t in range(ts):
            y = cb_ref[:, c0:c0 + FF_CHUNK]
            for tap in range(FFN_CONV_W):
                y = y + cw_ref[tap:tap + 1, c0:c0 + FF_CHUNK] * slabs[t + tap]
            outs.append(y)
        return jnp.concatenate(outs, axis=0)

    acc = None
    for c in range(dff // FF_CHUNK):
        gate = conv_cols(c * FF_CHUNK)
        lin = conv_cols(dff + c * FF_CHUNK)
        act = (gate * jax.nn.sigmoid(gate) * lin).astype(BF16)
        down = jnp.dot(act, wdn_ref[c * FF_CHUNK:(c + 1) * FF_CHUNK, :], preferred_element_type=F32)
        acc = down if acc is None else acc + down
    y_ref[...] = x1 + acc


def _ffn_sample(x_tm, oa, ob, oc, od, fbuf_tm, lp, consts, l):
    n, D = x_tm.shape
    dff = lp["w_down"].shape[1]
    ns = n // consts["ts"]
    layer = lambda *s: pl.BlockSpec((None,) + s, lambda i: (l,) + (0,) * len(s), pipeline_mode=pl.Buffered(1))
    const = lambda *s: pl.BlockSpec(s, lambda i: (0,) * len(s))
    return pl.pallas_call(
        functools.partial(_ffn_sample_kernel, ts=consts["ts"]),
        grid=(1,),
        in_specs=[const(n, D)] + [const(n, D_GRP)] * 4 + [layer(D, D), layer(1, D), layer(D, 2 * dff),
                  layer(8, 2 * dff), layer(1, 2 * dff), layer(dff, D), const(2, ns, 2 * dff)],
        out_specs=[const(n, D), const(2, ns, 2 * dff)],
        out_shape=[jax.ShapeDtypeStruct((n, D), F32), jax.ShapeDtypeStruct((2, ns, 2 * dff), F32)],
        compiler_params=pltpu.CompilerParams(dimension_semantics=("arbitrary",), vmem_limit_bytes=VMEM_LIMIT),
        name="ffn_sample",
    )(x_tm, oa, ob, oc, od, lp["w_out"], lp["norm_ffn"], lp["w_up"], lp["ffn_conv_w"], lp["ffn_conv_b"],
      lp["w_down"], fbuf_tm)


def _pad_rows(a, rows):
    return jnp.pad(a, ((0, 0), (0, rows - a.shape[1]), (0, 0)))


def _rope_tables(pos):
    half = HEAD_DIM // 2
    inv = ROPE_THETA ** (-jnp.arange(half, dtype=F32) / half)
    ang = pos.astype(F32)[:, None] * inv[None, :]
    cos = jnp.tile(jnp.cos(ang), (1, 2 * N_HEADS))
    sin = jnp.sin(ang)
    sin_signed = jnp.tile(jnp.concatenate([-sin, sin], axis=1), (1, N_HEADS))
    return cos, sin_signed


def _prepare(norm_mix, w_in, sgu_norm, sgu_w, sgu_b, moba_qn, moba_kn, fox_qn, fox_kn, fox_fb, conv_w, conv_b,
             conv_ln_g, conv_ln_b, w_out, norm_ffn, w_up, ffn_conv_w, ffn_conv_b, w_down, ts):
    depth = w_in.shape[0]
    n_main = 8 * D_GRP
    w_in_p = jnp.concatenate([w_in[:, :, :n_main], w_in[:, :, n_main + N_HEADS:], w_in[:, :, n_main:n_main + N_HEADS],
                              jnp.zeros(w_in.shape[:2] + (LANES - N_HEADS,), w_in.dtype)], axis=-1).astype(BF16)
    per_head = lambda g: jnp.tile(g, (1, N_HEADS))
    gains = jnp.stack([sgu_norm.reshape(depth, D_GRP), per_head(moba_qn), per_head(moba_kn), per_head(fox_qn),
                       per_head(fox_kn)], axis=1)
    lanes_of_head = lambda a: jnp.repeat(a, HEAD_DIM, axis=-1)
    return {
        "norm_mix": norm_mix[:, None, :],
        "w_in": w_in_p,
        "gains": _pad_rows(gains, 8),
        "fb": jnp.pad(fox_fb, ((0, 0), (0, LANES - N_HEADS)))[:, None, :],
        "sgu_w": sgu_w,
        "sgu_bmix": lanes_of_head(jnp.swapaxes(sgu_b, 1, 2)),
        "sgu_w_s": lanes_of_head(jnp.transpose(sgu_w[:, :, :ts, :ts], (0, 2, 3, 1))),
        "sgu_b_s": lanes_of_head(jnp.swapaxes(sgu_b[:, :, :ts], 1, 2)),
        "conv_w": _pad_rows(conv_w, 32),
        "conv_b": conv_b[:, None, :],
        "conv_ln_g": conv_ln_g[:, None, :],
        "conv_ln_b": conv_ln_b[:, None, :],
        "w_out": w_out.astype(BF16),
        "norm_ffn": norm_ffn[:, None, :],
        "w_up": w_up.astype(BF16),
        "ffn_conv_w": _pad_rows(ffn_conv_w, 8),
        "ffn_conv_b": ffn_conv_b[:, None, :],
        "w_down": w_down.astype(BF16),
    }


def _constants(T, ts, ns, past_len, n_pages):
    blk = np.arange(D_GRP) // HEAD_DIM
    bd = jnp.asarray((blk[:, None] == blk[None, :]).astype(np.float32) / HEAD_DIM, dtype=BF16)
    cos_p, sin_p = _rope_tables(jnp.arange(T, dtype=jnp.int32))
    pos_s = past_len + jnp.repeat(jnp.arange(ts, dtype=jnp.int32), ns)
    cos_s, sin_s = _rope_tables(pos_s)
    rows = np.arange(n_pages * 8 * N_HEADS)
    src = (rows // (8 * N_HEADS)) * N_HEADS + (rows % (8 * N_HEADS)) // 8
    expand = jnp.asarray((src[:, None] == np.arange(n_pages * N_HEADS)[None, :]).astype(np.float32))
    return {"bd": bd, "cos_p": cos_p, "sin_p": sin_p, "cos_s": cos_s, "sin_s": sin_s, "expand": expand, "ts": ts}


def kernel(x_prompt, x_sample, cache_moba_k, cache_moba_v, cache_fox_k, cache_fox_v, cache_fox_logf, state_conv, state_ffn_conv, page_table, norm_mix, w_in, sgu_norm, sgu_w, sgu_b, moba_qn, moba_kn, fox_qn, fox_kn, fox_fb, conv_w, conv_b, conv_ln_g, conv_ln_b, w_out, norm_ffn, w_up, ffn_conv_w, ffn_conv_b, w_down):
    B, T, D = x_prompt.shape
    ns, ts, _ = x_sample.shape
    depth = w_in.shape[0]
    n_pages = page_table.shape[1]
    page = cache_moba_k.shape[2]
    past_len = n_pages * page
    assert T % ATT_BLOCK == 0 and T % TOKEN_TILE == 0 and T % FFN_TILE == 0
    assert past_len % MOBA_BLOCK == 0 and ts <= 8

    lp = _prepare(norm_mix, w_in, sgu_norm, sgu_w, sgu_b, moba_qn, moba_kn, fox_qn, fox_kn, fox_fb, conv_w, conv_b,
                  conv_ln_g, conv_ln_b, w_out, norm_ffn, w_up, ffn_conv_w, ffn_conv_b, w_down, ts)
    consts = _constants(T, ts, ns, past_len, n_pages)

    chan_major = lambda c: jnp.transpose(c, (0, 1, 3, 4, 2)).reshape(c.shape[0], c.shape[1], D_GRP, c.shape[2])
    pools = (chan_major(cache_moba_k), chan_major(cache_moba_v), chan_major(cache_fox_k), chan_major(cache_fox_v),
             jnp.transpose(cache_fox_logf, (0, 1, 3, 2)))
    hist_tm = jnp.transpose(state_conv, (0, 2, 1, 3))
    fbuf_tm = jnp.transpose(state_ffn_conv, (0, 2, 1, 3))

    xp = x_prompt
    xs = jnp.transpose(x_sample, (1, 0, 2)).reshape(ts * ns, D)
    P = {k: [] for k in ("conv", "ffn")}
    S = {k: [] for k in ("kb", "vb", "kc", "vc", "lf", "conv", "ffn", "sv")}
    carried = [jnp.zeros((depth, B, D_GRP, T), F32) for _ in range(4)] + [jnp.zeros((depth, B, N_HEADS, T), F32)]
    for l in range(depth):
        carried, (oa, glu, qbT, qcT16, kb16, kc16, vbT16, vcT16, km) = _inproj_prompt(xp, lp, consts, l, carried)
        ob, oc = _attn_prompt(qbT, qcT16, kb16, vbT16, kc16, vcT16, km, carried[4], l)
        od, conv_tail = _conv_prompt(glu, lp, l)
        xp, ffn_tail = _ffn_prompt(xp, oa, ob, oc, od, lp, l)
        P["conv"].append(conv_tail)
        P["ffn"].append(ffn_tail)

        soa, sod, sqb, skb, svb, sqc, skc, svc, slf, ssv, s_tail = _inproj_sample(xs, hist_tm, lp, consts, l)
        sob, soc = _attn_sample(page_table, sqb, skb, svb, sqc, skc, svc, slf, pools, consts, l)
        xs, sffn_tail = _ffn_sample(xs, soa, sob, soc, sod, fbuf_tm[l], lp, consts, l)
        for k, v in zip(("kb", "vb", "kc", "vc", "lf", "conv", "ffn", "sv"),
                        (skb, svb, skc, svc, slf, s_tail, sffn_tail, ssv)):
            S[k].append(v)

    P = {k: jnp.stack(v) for k, v in P.items()}
    P.update(zip(("kb", "vb", "kc", "vc", "lf"), carried))
    S = {k: jnp.stack(v) for k, v in S.items()}
    heads_p = lambda a: jnp.transpose(a.reshape(depth, B, N_HEADS, HEAD_DIM, T), (0, 1, 4, 2, 3))
    heads_s = lambda a: jnp.transpose(a.reshape(depth, ts, ns, N_HEADS, HEAD_DIM), (0, 2, 1, 3, 4))
    return (xp, jnp.transpose(xs.reshape(ts, ns, D), (1, 0, 2)),
            heads_p(P["kb"]), heads_p(P["vb"]), heads_p(P["kc"]), heads_p(P["vc"]),
            jnp.transpose(P["lf"], (0, 1, 3, 2)), P["conv"], P["ffn"],
            heads_s(S["kb"]), heads_s(S["vb"]), heads_s(S["kc"]), heads_s(S["vc"]),
            jnp.transpose(S["lf"].reshape(depth, ts, ns, LANES)[..., :N_HEADS], (0, 2, 1, 3)),
            jnp.transpose(S["conv"], (0, 2, 1, 3)), jnp.transpose(S["ffn"], (0, 2, 1, 3)),
            jnp.transpose(S["sv"].reshape(depth, ts, ns, D_GRP), (0, 2, 1, 3)))
```

```python
import functools

import numpy as np
import jax
import jax.numpy as jnp
from jax import lax
from jax.experimental import pallas as pl
from jax.experimental.pallas import tpu as pltpu

F32 = jnp.float32
BF16 = jnp.bfloat16
HIGHEST = lax.Precision.HIGHEST

HEAD_DIM = 64
N_HEADS = 4
D_GRP = N_HEADS * HEAD_DIM
SGU_CHUNK = 128
MOBA_BLOCK = 256
MOBA_TOPK = 3
CONV_W = 31
FFN_CONV_W = 3
ROPE_THETA = 10000.0
EPS = 1e-6
NEG_INF = -1e30
LANES = 128
Q_SCALE = HEAD_DIM ** -0.5
LOG2E = 1.4426950408889634
ACC_ROWS = HEAD_DIM + 16

TOKEN_TILE = 256
FFN_TILE = 512
ATT_BLOCK = 512
FF_CHUNK = 256
CONV_ROWS = 64
VMEM_LIMIT = 52 * 1024 * 1024


def _gelu(x):
    return 0.5 * x * (1.0 + jnp.tanh(0.7978845608028654 * (x + 0.044715 * (x * x * x))))


def _log_sigmoid(x):
    return jnp.minimum(x, 0.0) - jnp.log1p(jnp.exp(-jnp.abs(x)))


def _rms(x, gain):
    ms = jnp.mean(x * x, axis=-1, keepdims=True)
    return x * lax.rsqrt(ms + EPS) * gain


def _head_rms(y, gain, bd):
    sq = y * y
    hi = sq.astype(BF16)
    lo = (sq - hi.astype(F32)).astype(BF16)
    ms = jnp.dot(hi, bd, preferred_element_type=F32) + jnp.dot(lo, bd, preferred_element_type=F32)
    return y * lax.rsqrt(ms + EPS) * gain


def _rope(y, cos, sin_signed):
    lane = lax.broadcasted_iota(jnp.int32, y.shape, 1)
    first = (lane % HEAD_DIM) < (HEAD_DIM // 2)
    partner = jnp.where(first, pltpu.roll(y, D_GRP - HEAD_DIM // 2, 1), pltpu.roll(y, HEAD_DIM // 2, 1))
    return y * cos + partner * sin_signed


def _head_of_lane(shape):
    return lax.broadcasted_iota(jnp.int32, shape, len(shape) - 1) // HEAD_DIM


def _in_groups(h, w_ref, bd, gains, fb, cos, sin_s):
    def grp(g, width=D_GRP):
        return jnp.dot(h, w_ref[:, g * D_GRP:g * D_GRP + width], preferred_element_type=F32)

    u = _gelu(grp(0))
    sv = _head_rms(_gelu(grp(1)), gains[0:1], bd)
    qb = _rope(_head_rms(grp(2), gains[1:2], bd), cos, sin_s) * Q_SCALE
    kb = _rope(_head_rms(grp(3), gains[2:3], bd), cos, sin_s)
    vb = grp(4)
    qc = _head_rms(grp(5), gains[3:4], bd) * Q_SCALE
    kc = _head_rms(grp(6), gains[4:5], bd)
    vc = grp(7)
    glu = grp(8) * jax.nn.sigmoid(grp(9))
    logf = _log_sigmoid(grp(10, LANES) + fb)
    return u, sv, qb, kb, vb, qc, kc, vc, logf, glu


def _layer_norm_silu(y, g, b):
    mu = jnp.mean(y, axis=-1, keepdims=True)
    yc = y - mu
    var = jnp.mean(yc * yc, axis=-1, keepdims=True)
    z = yc * lax.rsqrt(var + EPS) * g + b
    return z * jax.nn.sigmoid(z)


def _inproj_prompt_kernel(x_ref, gmix_ref, w_ref, bd_ref, gains_ref, fb_ref, cos_ref, sin_ref, wmix_ref, bmix_ref,
                          *refs, n_carried):
    (kbT_ref, vbT_ref, kcT_ref, vcT_ref, lfT_ref, oa_ref, glu_ref,
     qbT_ref, qcT16_ref, kb16_ref, kc16_ref, vbT16_ref, vcT16_ref, km_ref) = refs[n_carried:]
    h = _rms(x_ref[...], gmix_ref[...]).astype(BF16)
    u, sv, qb, kb, vb, qc, kc, vc, logf, glu = _in_groups(
        h, w_ref, bd_ref[...], gains_ref[...], fb_ref[...], cos_ref[...], sin_ref[...])
    glu_ref[...] = glu
    qbT_ref[...] = (qb * LOG2E).T
    qcT16_ref[...] = (qc * LOG2E).T.astype(BF16)
    kb16_ref[...] = kb.astype(BF16)
    kc16_ref[...] = kc.astype(BF16)
    kbT_ref[...] = kb.T
    kcT_ref[...] = kc.T
    for src, full_ref, half_ref in ((vb, vbT_ref, vbT16_ref), (vc, vcT_ref, vcT16_ref)):
        srcT = src.T
        full_ref[...] = srcT
        half_ref[...] = srcT.astype(BF16)
    lfT_ref[...] = logf.T[0:N_HEADS, :]
    for blk in range(km_ref.shape[0]):
        mean = jnp.mean(kb[blk * MOBA_BLOCK:(blk + 1) * MOBA_BLOCK], axis=0, keepdims=True)
        km_ref[blk] = jnp.broadcast_to(mean, (8, D_GRP))

    n = SGU_CHUNK
    row = lax.broadcasted_iota(jnp.int32, (n, n), 0)
    col = lax.broadcasted_iota(jnp.int32, (n, n), 1)
    head = _head_of_lane((n, D_GRP))
    svb = sv.astype(BF16)
    for c in range(x_ref.shape[0] // n):
        svc = svb[c * n:(c + 1) * n]
        mixed = bmix_ref[...]
        for g in range(N_HEADS):
            wg = jnp.where(row >= col, wmix_ref[g], 0.0).astype(BF16)
            mixed = mixed + jnp.where(head == g, jnp.dot(wg, svc, preferred_element_type=F32), 0.0)
        oa_ref[c * n:(c + 1) * n, :] = u[c * n:(c + 1) * n] * mixed


def _inproj_prompt(x, lp, consts, l, carried):
    B, T, D = x.shape
    depth = lp["w_in"].shape[0]
    tm = TOKEN_TILE
    nt = T // tm
    din = lp["w_in"].shape[-1]
    tok = lambda w: pl.BlockSpec((None, tm, w), lambda b, i: (b, i, 0))
    chan = lambda r: pl.BlockSpec((None, r, tm), lambda b, i: (b, 0, i))
    chan_l = lambda r: pl.BlockSpec((None, None, r, tm), lambda b, i: (l, b, 0, i))
    layer = lambda *s: pl.BlockSpec((None,) + s, lambda b, i: (l,) + (0,) * len(s))
    const = lambda *s: pl.BlockSpec(s, lambda b, i: (0,) * len(s))
    carried = tuple(carried)
    n_in = 10
    outs = pl.pallas_call(
        functools.partial(_inproj_prompt_kernel, n_carried=len(carried)),
        grid=(B, nt),
        in_specs=[tok(D), layer(1, D), layer(D, din), const(D_GRP, D_GRP), layer(8, D_GRP), layer(1, LANES),
                  pl.BlockSpec((tm, D_GRP), lambda b, i: (i, 0)), pl.BlockSpec((tm, D_GRP), lambda b, i: (i, 0)),
                  layer(N_HEADS, SGU_CHUNK, SGU_CHUNK), layer(SGU_CHUNK, D_GRP)]
                 + [pl.BlockSpec(memory_space=pl.ANY)] * len(carried),
        out_specs=[chan_l(D_GRP), chan_l(D_GRP), chan_l(D_GRP), chan_l(D_GRP), chan_l(N_HEADS), tok(D_GRP), tok(D_GRP),
                   chan(D_GRP), chan(D_GRP), tok(D_GRP), tok(D_GRP), chan(D_GRP), chan(D_GRP),
                   pl.BlockSpec((None, tm // MOBA_BLOCK, 8, D_GRP), lambda b, i: (b, i, 0, 0))],
        out_shape=[jax.ShapeDtypeStruct((depth, B, D_GRP, T), F32)] * 4
                  + [jax.ShapeDtypeStruct((depth, B, N_HEADS, T), F32),
                     jax.ShapeDtypeStruct((B, T, D_GRP), F32), jax.ShapeDtypeStruct((B, T, D_GRP), F32),
                     jax.ShapeDtypeStruct((B, D_GRP, T), F32), jax.ShapeDtypeStruct((B, D_GRP, T), BF16),
                     jax.ShapeDtypeStruct((B, T, D_GRP), BF16), jax.ShapeDtypeStruct((B, T, D_GRP), BF16),
                     jax.ShapeDtypeStruct((B, D_GRP, T), BF16), jax.ShapeDtypeStruct((B, D_GRP, T), BF16),
                     jax.ShapeDtypeStruct((B, T // MOBA_BLOCK, 8, D_GRP), F32)],
        input_output_aliases={n_in + k: k for k in range(len(carried))},
        compiler_params=pltpu.CompilerParams(dimension_semantics=("arbitrary", "arbitrary"),
                                             vmem_limit_bytes=VMEM_LIMIT),
        name="inproj_prompt",
    )(x, lp["norm_mix"], lp["w_in"], consts["bd"], lp["gains"], lp["fb"], consts["cos_p"], consts["sin_p"],
      lp["sgu_w"], lp["sgu_bmix"], *carried)
    return outs[:5], outs[5:]


def _pair_tables(nq):
    qi, kj = [], []
    for i in range(nq):
        qi.append(i)
        kj.append(i)
        for j in range(i):
            qi.append(i)
            kj.append(j)
    return np.asarray(qi, np.int32), np.asarray(kj, np.int32)


def _attn_prompt_kernel(qi_ref, kj_ref, qbT_ref, qcT_ref, kb_ref, vbT_ref, kc_ref, vcT_ref, km_ref, lfT_ref,
                        ob_ref, oc_ref, m_scr, acc_scr, qm_scr, crow_scr, ccol_scr, bias_scr, cmask_scr):
    s_idx = pl.program_id(1)
    i = qi_ref[s_idx]
    j = kj_ref[s_idx]
    tq = qbT_ref.shape[1]
    tk = kb_ref.shape[0]
    nblk = km_ref.shape[0]
    nsub = tk // MOBA_BLOCK
    diag = j == i
    last = jnp.logical_or(j == i - 1, i == 0)

    @pl.when(s_idx == 0)
    def _():
        crow_scr[...] = jnp.zeros_like(crow_scr)
        r = lax.broadcasted_iota(jnp.int32, (LANES, LANES), 0)
        c = lax.broadcasted_iota(jnp.int32, (LANES, LANES), 1)
        upper = jnp.where(r <= c, 1.0, 0.0).astype(F32)
        carry = jnp.zeros((N_HEADS, 1), F32)
        per_blk = tk // LANES
        for cc in range(lfT_ref.shape[1] // LANES):
            w = jnp.dot(lfT_ref[:, cc * LANES:(cc + 1) * LANES], upper, precision=HIGHEST,
                        preferred_element_type=F32) + carry
            crow_scr[cc // per_blk, 0:N_HEADS, (cc % per_blk) * LANES:(cc % per_blk + 1) * LANES] = w
            carry = w[:, LANES - 1:LANES]
        for blk in range(crow_scr.shape[0]):
            for h in range(N_HEADS):
                ccol_scr[blk, h] = jnp.broadcast_to(crow_scr[blk, h:h + 1, :] * LOG2E, (LANES, tk)).T
        cmask_scr[...] = jnp.where(lax.broadcasted_iota(jnp.int32, (tk, tq), 0)
                                   <= lax.broadcasted_iota(jnp.int32, (tk, tq), 1), 0.0, NEG_INF)

    def attend(a, k_ref, vT_ref, mask):
        s = jnp.dot(k_ref[...], qm_scr[a], preferred_element_type=F32)
        s = mask(s, a)
        m_old = m_scr[a]
        m_new = jnp.maximum(m_old, jnp.max(s, axis=0, keepdims=True))
        alpha = jnp.exp2(m_old - m_new)
        p16 = jnp.exp2(s - m_new).astype(BF16)
        m_scr[a] = m_new
        ones = jnp.ones((ACC_ROWS - HEAD_DIM, tk), BF16)
        for h in range(N_HEADS):
            slot = N_HEADS * a + h
            v1 = jnp.concatenate([vT_ref[h * HEAD_DIM:(h + 1) * HEAD_DIM, :], ones], axis=0)
            pv = jnp.dot(v1, p16[:, h * tq:(h + 1) * tq], preferred_element_type=F32)
            acc_scr[slot] = acc_scr[slot] * alpha[:, h * tq:(h + 1) * tq] + pv

    def both(mask):
        attend(0, kb_ref, vbT_ref, mask)
        attend(1, kc_ref, vcT_ref, mask)

    def minus_c(s):
        return s - jnp.concatenate([ccol_scr[j, h] for h in range(N_HEADS) for _ in range(tq // LANES)], axis=1)

    def plus_block_bias(s):
        return jnp.concatenate([s[k * MOBA_BLOCK:(k + 1) * MOBA_BLOCK] + bias_scr[pl.ds(nsub * j + k, 1), :]
                                for k in range(nsub)], axis=0)

    @pl.when(diag)
    def _():
        m_scr[...] = jnp.full_like(m_scr, NEG_INF)
        acc_scr[...] = jnp.zeros_like(acc_scr)
        qbT = qbT_ref[...]
        qbT16 = qbT.astype(BF16)
        qcT16 = qcT_ref[...]
        head_r = lax.broadcasted_iota(jnp.int32, (D_GRP, tq), 0) // HEAD_DIM
        for h in range(N_HEADS):
            qm_scr[0, :, h * tq:(h + 1) * tq] = jnp.where(head_r == h, qbT16, jnp.zeros_like(qbT16))
            qm_scr[1, :, h * tq:(h + 1) * tq] = jnp.where(head_r == h, qcT16, jnp.zeros_like(qcT16))

        sub = lax.broadcasted_iota(jnp.int32, (8, D_GRP), 0)
        km = jnp.zeros((8, D_GRP), F32)
        for n in range(nblk):
            km = jnp.where(sub == n, km_ref[n], km)
        head_l = _head_of_lane((8, D_GRP))
        blk = lax.broadcasted_iota(jnp.int32, (8, tq), 0)
        own = nsub * i + lax.broadcasted_iota(jnp.int32, (1, tq), 1) // MOBA_BLOCK
        for h in range(N_HEADS):
            gate = jnp.dot(jnp.where(head_l == h, km, 0.0), qbT, precision=HIGHEST,
                           preferred_element_type=F32)
            rank = jnp.zeros((8, tq), F32)
            for mblk in range(nblk - 1):
                gm = gate[mblk:mblk + 1, :]
                ahead = jnp.where(blk > mblk, jnp.where(gm >= gate, 1.0, 0.0), jnp.where(gm > gate, 1.0, 0.0))
                rank = rank + ahead * jnp.where(mblk < own, 1.0, 0.0)
            keep = ((rank < MOBA_TOPK) & (blk < own)) | (blk == own)
            bias_scr[:, h * tq:(h + 1) * tq] = jnp.where(keep, 0.0, NEG_INF)

        causal = jnp.concatenate([cmask_scr[...]] * N_HEADS, axis=1)

        def mask(s, a):
            s = minus_c(s) if a == 1 else plus_block_bias(s)
            return s + causal

        both(mask)

    @pl.when(jnp.logical_not(diag))
    def _():
        def mask(s, a):
            return minus_c(s) if a == 1 else plus_block_bias(s)

        both(mask)

    @pl.when(last)
    def _():
        for a, o_ref in enumerate((ob_ref, oc_ref)):
            heads = []
            for h in range(N_HEADS):
                acc = acc_scr[N_HEADS * a + h]
                heads.append(acc[0:HEAD_DIM] / acc[HEAD_DIM:HEAD_DIM + 1])
            o_ref[...] = jnp.concatenate(heads, axis=0).T


def _attn_prompt(qbT, qcT16, kb16, vbT16, kc16, vcT16, km, lfT_all, l):
    B, _, T = qbT.shape
    tq = ATT_BLOCK
    nq = T // tq
    nblk = T // MOBA_BLOCK
    assert nblk <= 8 and tq % MOBA_BLOCK == 0
    qi, kj = _pair_tables(nq)
    qspec = pl.BlockSpec((None, D_GRP, tq), lambda b, s, qi, kj: (b, 0, qi[s]))
    kspec = pl.BlockSpec((None, tq, D_GRP), lambda b, s, qi, kj: (b, kj[s], 0))
    vspec = pl.BlockSpec((None, D_GRP, tq), lambda b, s, qi, kj: (b, 0, kj[s]))
    grid_spec = pltpu.PrefetchScalarGridSpec(
        num_scalar_prefetch=2,
        grid=(B, len(qi)),
        in_specs=[qspec, qspec, kspec, vspec, kspec, vspec,
                  pl.BlockSpec((None, nblk, 8, D_GRP), lambda b, s, qi, kj: (b, 0, 0, 0)),
                  pl.BlockSpec((None, None, N_HEADS, T), lambda b, s, qi, kj: (l, b, 0, 0))],
        out_specs=[pl.BlockSpec((None, tq, D_GRP), lambda b, s, qi, kj: (b, qi[s], 0))] * 2,
        scratch_shapes=[pltpu.VMEM((2, 1, N_HEADS * tq), F32),
                        pltpu.VMEM((2 * N_HEADS, ACC_ROWS, tq), F32), pltpu.VMEM((2, D_GRP, N_HEADS * tq), BF16),
                        pltpu.VMEM((nq, 8, tq), F32), pltpu.VMEM((nq, N_HEADS, tq, LANES), F32),
                        pltpu.VMEM((8, N_HEADS * tq), F32), pltpu.VMEM((tq, tq), F32)],
    )
    return pl.pallas_call(
        _attn_prompt_kernel,
        grid_spec=grid_spec,
        out_shape=[jax.ShapeDtypeStruct((B, T, D_GRP), F32)] * 2,
        compiler_params=pltpu.CompilerParams(dimension_semantics=("arbitrary", "arbitrary"),
                                             vmem_limit_bytes=VMEM_LIMIT),
        name="attn_prompt",
    )(jnp.asarray(qi), jnp.asarray(kj), qbT, qcT16, kb16, vbT16, kc16, vcT16, km, lfT_all)


def _conv_prompt_kernel(x_ref, w_ref, b_ref, g_ref, beta_ref, o_ref, tail_ref, s_scr):
    T = x_ref.shape[0]
    pad = 32
    base = pad - (CONV_W - 1)
    s_scr[0, 0:pad, :] = jnp.zeros((pad, D_GRP), F32)
    s_scr[0, pad:pad + T, :] = x_ref[...]
    tail_ref[...] = s_scr[0, pad + T - (CONV_W - 1):pad + T, :]
    n_rows = T + pad - 8
    piece = n_rows // 7
    assert piece * 7 == n_rows and piece % 8 == 0
    for r in range(1, 8):
        for a in range(0, n_rows, piece):
            s_scr[r, a:a + piece, :] = s_scr[0, a + r:a + r + piece, :]
    w = w_ref[...]
    R = CONV_ROWS

    def chunk(c, _):
        r0 = pl.multiple_of(c * R, R)
        acc = jnp.zeros((R, D_GRP), F32) + b_ref[...]
        for tap in range(CONV_W):
            off = base + tap
            acc = acc + w[tap:tap + 1, :] * s_scr[off % 8, pl.ds(pl.multiple_of(r0 + off - off % 8, 8), R), :]
        o_ref[pl.ds(r0, R), :] = _layer_norm_silu(acc, g_ref[...], beta_ref[...])
        return 0

    lax.fori_loop(0, T // R, chunk, 0)


def _conv_prompt(glu, lp, l):
    B, T, _ = glu.shape
    layer = lambda *s: pl.BlockSpec((None,) + s, lambda b: (l,) + (0,) * len(s))
    return pl.pallas_call(
        _conv_prompt_kernel,
        grid=(B,),
        in_specs=[pl.BlockSpec((None, T, D_GRP), lambda b: (b, 0, 0)), layer(32, D_GRP), layer(1, D_GRP),
                  layer(1, D_GRP), layer(1, D_GRP)],
        out_specs=[pl.BlockSpec((None, T, D_GRP), lambda b: (b, 0, 0)),
                   pl.BlockSpec((None, CONV_W - 1, D_GRP), lambda b: (b, 0, 0))],
        out_shape=[jax.ShapeDtypeStruct((B, T, D_GRP), F32), jax.ShapeDtypeStruct((B, CONV_W - 1, D_GRP), F32)],
        scratch_shapes=[pltpu.VMEM((8, T + 32, D_GRP), F32)],
        compiler_params=pltpu.CompilerParams(dimension_semantics=("arbitrary",), vmem_limit_bytes=VMEM_LIMIT),
        name="conv_prompt",
    )(glu, lp["conv_w"], lp["conv_b"], lp["conv_ln_g"], lp["conv_ln_b"])


def _ffn_prompt_kernel(x_ref, oa_ref, ob_ref, oc_ref, od_ref, wo_ref, gffn_ref, wup_ref, cw_ref, cb_ref, wdn_ref,
                       y_ref, tail_ref, s_scr):
    t_idx = pl.program_id(1)
    tm = x_ref.shape[0]
    dff = wdn_ref.shape[0]
    nchunk = dff // FF_CHUNK

    @pl.when(t_idx == 0)
    def _():
        for k in range(2 * nchunk):
            s_scr[k, 0:8, :] = jnp.zeros((8, FF_CHUNK), F32)

    mix = None
    for g, ref in enumerate((oa_ref, ob_ref, oc_ref, od_ref)):
        part = jnp.dot(ref[...].astype(BF16), wo_ref[g * D_GRP:(g + 1) * D_GRP, :], preferred_element_type=F32)
        mix = part if mix is None else mix + part
    x1 = x_ref[...] + mix
    hf = _rms(x1, gffn_ref[...]).astype(BF16)

    def conv_cols(k, c0):
        up = jnp.dot(hf, wup_ref[:, c0:c0 + FF_CHUNK], preferred_element_type=F32)
        s_scr[k, 8:8 + tm, :] = up
        y = (cw_ref[0:1, c0:c0 + FF_CHUNK] * s_scr[k, 6:6 + tm, :]
             + cw_ref[1:2, c0:c0 + FF_CHUNK] * s_scr[k, 7:7 + tm, :]
             + cw_ref[2:3, c0:c0 + FF_CHUNK] * up + cb_ref[:, c0:c0 + FF_CHUNK])
        last2 = s_scr[k, tm + 6:tm + 8, :]
        s_scr[k, 6:8, :] = last2
        tail_ref[:, c0:c0 + FF_CHUNK] = last2
        return y

    acc = None
    for c in range(nchunk):
        gate = conv_cols(2 * c, c * FF_CHUNK)
        lin = conv_cols(2 * c + 1, dff + c * FF_CHUNK)
        act = (gate * jax.nn.sigmoid(gate) * lin).astype(BF16)
        down = jnp.dot(act, wdn_ref[c * FF_CHUNK:(c + 1) * FF_CHUNK, :], preferred_element_type=F32)
        acc = down if acc is None else acc + down
    y_ref[...] = x1 + acc


def _ffn_prompt(x, oa, ob, oc, od, lp, l):
    B, T, D = x.shape
    tm = FFN_TILE
    dff = lp["w_down"].shape[1]
    tok = lambda w: pl.BlockSpec((None, tm, w), lambda b, i: (b, i, 0))
    layer = lambda *s: pl.BlockSpec((None,) + s, lambda b, i: (l,) + (0,) * len(s))
    once =lambda *s: pl.BlockSpec((None,) + s, lambda b, i: (l,) + (0,) * len(s), pipeline_mode=pl.Buffered(1))
    return pl.pallas_call(
        _ffn_prompt_kernel,
        grid=(B, T // tm),
        in_specs=[tok(D), tok(D_GRP), tok(D_GRP), tok(D_GRP), tok(D_GRP), once(D, D), layer(1, D),
                  once(D, 2 * dff), layer(8, 2 * dff), layer(1, 2 * dff), once(dff, D)],
        out_specs=[tok(D), pl.BlockSpec((None, 2, 2 * dff), lambda b, i: (b, 0, 0))],
        out_shape=[jax.ShapeDtypeStruct((B, T, D), F32), jax.ShapeDtypeStruct((B, 2, 2 * dff), F32)],
        scratch_shapes=[pltpu.VMEM((2 * dff // FF_CHUNK, tm + 8, FF_CHUNK), F32)],
        compiler_params=pltpu.CompilerParams(dimension_semantics=("arbitrary", "arbitrary"),
                                             vmem_limit_bytes=VMEM_LIMIT),
        name="ffn_prompt",
    )(x, oa, ob, oc, od, lp["w_out"], lp["norm_ffn"], lp["w_up"], lp["ffn_conv_w"], lp["ffn_conv_b"], lp["w_down"])


def _inproj_sample_kernel(x_ref, gmix_ref, w_ref, bd_ref, gains_ref, fb_ref, cos_ref, sin_ref, wmix_ref, bmix_ref,
                          hist_ref, cw_ref, cb_ref, g_ref, beta_ref,
                          oa_ref, od_ref, qb_ref, kb_ref, vb_ref, qc_ref, kc_ref, vc_ref, lf_ref, sv_ref, tail_ref):
    ts = wmix_ref.shape[0]
    ns = x_ref.shape[0] // ts
    h = _rms(x_ref[...], gmix_ref[...]).astype(BF16)
    u, sv, qb, kb, vb, qc, kc, vc, logf, glu = _in_groups(
        h, w_ref, bd_ref[...], gains_ref[...], fb_ref[...], cos_ref[...], sin_ref[...])
    qb_ref[...] = qb
    kb_ref[...] = kb
    vb_ref[...] = vb
    qc_ref[...] = qc
    kc_ref[...] = kc
    vc_ref[...] = vc
    lf_ref[...] = logf
    sv_ref[...] = sv

    for t in range(ts):
        mixed = bmix_ref[t:t + 1, :]
        for s in range(t + 1):
            mixed = mixed + wmix_ref[t, s:s + 1, :] * sv[s * ns:(s + 1) * ns]
        oa_ref[t * ns:(t + 1) * ns, :] = u[t * ns:(t + 1) * ns] * mixed

    nh = hist_ref.shape[0]
    slabs = [hist_ref[r] for r in range(nh)] + [glu[t * ns:(t + 1) * ns] for t in range(ts)]
    for r in range(nh):
        tail_ref[r] = slabs[ts + r]
    for t in range(ts):
        acc = jnp.zeros((ns, D_GRP), F32) + cb_ref[...]
        for tap in range(CONV_W):
            acc = acc + cw_ref[tap:tap + 1, :] * slabs[t + tap]
        od_ref[t * ns:(t + 1) * ns, :] = _layer_norm_silu(acc, g_ref[...], beta_ref[...])


def _inproj_sample(x_tm, hist_tm, lp, consts, l):
    n, D = x_tm.shape
    ts, ns = consts["ts"], n // consts["ts"]
    din = lp["w_in"].shape[-1]
    layer = lambda *s: pl.BlockSpec((None,) + s, lambda i: (l,) + (0,) * len(s))
    const = lambda *s: pl.BlockSpec(s, lambda i: (0,) * len(s))
    tok = jax.ShapeDtypeStruct((n, D_GRP), F32)
    nh = CONV_W - 1
    return pl.pallas_call(
        _inproj_sample_kernel,
        grid=(1,),
        in_specs=[const(n, D), layer(1, D), layer(D, din), const(D_GRP, D_GRP), layer(8, D_GRP), layer(1, LANES),
                  const(n, D_GRP), const(n, D_GRP), layer(ts, ts, D_GRP), layer(ts, D_GRP),
                  layer(nh, ns, D_GRP), layer(32, D_GRP), layer(1, D_GRP), layer(1, D_GRP), layer(1, D_GRP)],
        out_specs=[const(n, D_GRP)] * 8 + [const(n, LANES), const(n, D_GRP), const(nh, ns, D_GRP)],
        out_shape=[tok] * 8 + [jax.ShapeDtypeStruct((n, LANES), F32), tok,
                               jax.ShapeDtypeStruct((nh, ns, D_GRP), F32)],
        compiler_params=pltpu.CompilerParams(dimension_semantics=("arbitrary",), vmem_limit_bytes=VMEM_LIMIT),
        name="inproj_sample",
    )(x_tm, lp["norm_mix"], lp["w_in"], consts["bd"], lp["gains"], lp["fb"], consts["cos_s"], consts["sin_s"],
      lp["sgu_w_s"], lp["sgu_b_s"], hist_tm, lp["conv_w"], lp["conv_b"], lp["conv_ln_g"], lp["conv_ln_b"])


def _attn_sample_kernel(pt_ref, qb_ref, kb_ref, vb_ref, qc_ref, kc_ref, vc_ref, lf_ref, expand_ref,
                        pkb_ref, pvb_ref, pkc_ref, pvc_ref, plf_ref,
                        ob_ref, oc_ref,
                        kb_buf, vb_buf, kc_buf, vc_buf, lf_buf, sem, *, layer, n_pages):
    b = pl.program_id(0)
    nb = pl.num_programs(0)
    ts = qb_ref.shape[0]
    page = pkb_ref.shape[3]
    past = n_pages * page
    pools = (pkb_ref, pvb_ref, pkc_ref, pvc_ref)
    bufs = (kb_buf, vb_buf, kc_buf, vc_buf)

    def copies(seq, slot):
        out = []
        for p in range(n_pages):
            phys = pt_ref[seq * n_pages + p]
            for a in range(4):
                out.append(pltpu.make_async_copy(pools[a].at[layer, phys],
                                                 bufs[a].at[slot, :, pl.ds(p * page, page)], sem.at[slot, a]))
            out.append(pltpu.make_async_copy(plf_ref.at[layer, phys],
                                             lf_buf.at[slot, pl.ds(p * N_HEADS, N_HEADS), :], sem.at[slot, 4]))
        return out

    slot = b % 2

    @pl.when(b == 0)
    def _():
        for cp in copies(0, 0):
            cp.start()

    @pl.when(b + 1 < nb)
    def _():
        for cp in copies(b + 1, 1 - slot):
            cp.start()

    for cp in copies(b, slot):
        cp.wait()

    rows = 8 * N_HEADS
    head_l = _head_of_lane((8, D_GRP))
    sub8 = lax.broadcasted_iota(jnp.int32, (8, D_GRP), 0)

    def rows8(ref, width=D_GRP):
        sub = lax.broadcasted_iota(jnp.int32, (8, width), 0)
        out = jnp.zeros((8, width), F32)
        for t in range(ts):
            out = jnp.where(sub == t, ref[t, pl.ds(b, 1), :], out)
        return out

    def q_rows(ref):
        q8 = rows8(ref)
        return jnp.concatenate([jnp.where(head_l == h, q8, 0.0) for h in range(N_HEADS)], axis=0)

    r_sub = lax.broadcasted_iota(jnp.int32, (rows, 8), 0) % 8
    c_new = lax.broadcasted_iota(jnp.int32, (rows, 8), 1)
    new_ok = (c_new <= r_sub) & (c_new < ts)
    new_ok = new_ok | ((r_sub >= ts) & (c_new == 0))

    def attend(vT, v_new, s_past, s_new):
        m = jnp.maximum(jnp.max(s_past, axis=1, keepdims=True), jnp.max(s_new, axis=1, keepdims=True))
        p_past = jnp.exp(s_past - m)
        p_new = jnp.exp(s_new - m)
        den = jnp.sum(p_past, axis=1, keepdims=True) + jnp.sum(p_new, axis=1, keepdims=True)
        o = lax.dot_general(p_past.astype(BF16), vT, (((1,), (1,)), ((), ())), preferred_element_type=F32)
        o = o + jnp.dot(p_new.astype(BF16), v_new.astype(BF16), preferred_element_type=F32)
        o = o / den
        out8 = jnp.zeros((8, D_GRP), F32)
        for h in range(N_HEADS):
            out8 = jnp.where(head_l == h, o[8 * h:8 * h + 8], out8)
        return out8

    def write(o_ref, out8):
        for t in range(ts):
            o_ref[t, pl.ds(b, 1), :] = out8[t:t + 1]

    qr = q_rows(qb_ref)
    kTf = kb_buf[slot]
    n_blk = past // MOBA_BLOCK
    lane_k = lax.broadcasted_iota(jnp.int32, (D_GRP, LANES), 1)
    km = jnp.zeros((D_GRP, LANES), F32)
    for nblk in range(n_blk):
        km = jnp.where(lane_k == nblk,
                       jnp.mean(kTf[:, nblk * MOBA_BLOCK:(nblk + 1) * MOBA_BLOCK], axis=1, keepdims=True), km)
    gate = jnp.dot(qr, km, precision=HIGHEST, preferred_element_type=F32)
    lane_g = lax.broadcasted_iota(jnp.int32, (rows, LANES), 1)
    rank = jnp.zeros((rows, LANES), F32)
    for mblk in range(n_blk):
        gm = gate[:, mblk:mblk + 1]
        rank = rank + jnp.where((gm > gate) | ((gm == gate) & (mblk < lane_g)), 1.0, 0.0)
    sel = jnp.where((rank < MOBA_TOPK) & (lane_g < n_blk), 1.0, 0.0)
    qrb = qr.astype(BF16)
    s_full = jnp.dot(qrb, kTf.astype(BF16), preferred_element_type=F32)
    s_past = jnp.concatenate(
        [jnp.where(sel[:, nblk:nblk + 1] > 0.5, s_full[:, nblk * MOBA_BLOCK:(nblk + 1) * MOBA_BLOCK], NEG_INF)
         for nblk in range(n_blk)], axis=1)
    k_new = rows8(kb_ref)
    v_new = rows8(vb_ref)
    s_new = lax.dot_general(qrb, k_new.astype(BF16), (((1,), (1,)), ((), ())), preferred_element_type=F32)
    s_new = jnp.where(new_ok, s_new, NEG_INF)
    write(ob_ref, attend(vb_buf[slot].astype(BF16), v_new, s_past, s_new))

    x = lf_buf[slot]
    r = lax.broadcasted_iota(jnp.int32, (page, page), 0)
    c = lax.broadcasted_iota(jnp.int32, (page, page), 1)
    upper = jnp.where(r <= c, 1.0, 0.0).astype(F32)
    within = jnp.dot(x, upper, precision=HIGHEST, preferred_element_type=F32)
    np4 = n_pages * N_HEADS
    rr = lax.broadcasted_iota(jnp.int32, (np4, np4), 0)
    cc = lax.broadcasted_iota(jnp.int32, (np4, np4), 1)
    same_head = (rr % N_HEADS) == (cc % N_HEADS)
    tot = jnp.broadcast_to(within[:, page - 1:page], (np4, page))
    before = jnp.dot(jnp.where(same_head & (cc < rr - rr % N_HEADS), 1.0, 0.0).astype(F32), tot,
                     precision=HIGHEST, preferred_element_type=F32)
    total = jnp.dot(jnp.where(same_head, 1.0, 0.0).astype(F32), tot, precision=HIGHEST, preferred_element_type=F32)
    after = total - (within + before)
    bias = jnp.dot(expand_ref[...], after, precision=HIGHEST, preferred_element_type=F32)
    lf8 = rows8(lf_ref, LANES)
    cum_rows = []
    run = jnp.zeros((1, LANES), F32)
    for t in range(ts):
        run = run + lf8[t:t + 1]
        cum_rows.append(run)
    cnew_cols = jnp.zeros((rows, 8), F32)
    for t in range(ts):
        col_t = jnp.concatenate([jnp.broadcast_to(cum_rows[t][:, h:h + 1], (8, 1)) for h in range(N_HEADS)],
                                axis=0)
        cnew_cols = jnp.where(c_new == t, col_t, cnew_cols)
    qr = q_rows(qc_ref)
    qrb = qr.astype(BF16)
    kTf = kc_buf[slot]
    s_full = jnp.dot(qrb, kTf.astype(BF16), preferred_element_type=F32)
    s_past = jnp.concatenate([s_full[:, p * page:(p + 1) * page] + bias[p * rows:(p + 1) * rows]
                              for p in range(n_pages)], axis=1)
    k_new = rows8(kc_ref)
    v_new = rows8(vc_ref)
    s_new = lax.dot_general(qrb, k_new.astype(BF16), (((1,), (1,)), ((), ())), preferred_element_type=F32)
    s_new = jnp.where(new_ok, s_new - cnew_cols, NEG_INF)
    write(oc_ref, attend(vc_buf[slot].astype(BF16), v_new, s_past, s_new))


def _attn_sample(page_table, qb, kb, vb, qc, kc, vc, lf, pools, consts, l):
    ns, n_pages = page_table.shape
    ts = consts["ts"]
    pkb, pvb, pkc, pvc, plf = pools
    page = pkb.shape[3]
    past = n_pages * page
    tm3 = lambda a: a.reshape(ts, ns, a.shape[-1])
    whole = lambda w: pl.BlockSpec((ts, ns, w), lambda b, pt: (0, 0, 0))
    anyspec = pl.BlockSpec(memory_space=pl.ANY)
    grid_spec = pltpu.PrefetchScalarGridSpec(
        num_scalar_prefetch=1,
        grid=(ns,),
        in_specs=[whole(D_GRP)] * 6 + [whole(LANES), pl.BlockSpec(consts["expand"].shape, lambda b, pt: (0, 0))]
                 + [anyspec] * 5,
        out_specs=[whole(D_GRP), whole(D_GRP)],
        scratch_shapes=[pltpu.VMEM((2, D_GRP, past), F32)] * 4
                       + [pltpu.VMEM((2, n_pages * N_HEADS, page), F32), pltpu.SemaphoreType.DMA((2, 5))],
    )
    ob, oc = pl.pallas_call(
        functools.partial(_attn_sample_kernel, layer=l, n_pages=n_pages),
        grid_spec=grid_spec,
        out_shape=[jax.ShapeDtypeStruct((ts, ns, D_GRP), F32)] * 2,
        compiler_params=pltpu.CompilerParams(dimension_semantics=("arbitrary",), vmem_limit_bytes=VMEM_LIMIT),
        name="attn_sample",
    )(page_table.reshape(-1), tm3(qb), tm3(kb), tm3(vb), tm3(qc), tm3(kc), tm3(vc), tm3(lf), consts["expand"],
      pkb, pvb, pkc, pvc, plf)
    return ob.reshape(ts * ns, D_GRP), oc.reshape(ts * ns, D_GRP)


def _ffn_sample_kernel(x_ref, oa_ref, ob_ref, oc_ref, od_ref, wo_ref, gffn_ref, wup_ref, cw_ref, cb_ref, wdn_ref,
                       fbuf_ref, y_ref, tail_ref, *, ts):
    n = x_ref.shape[0]
    ns = n // ts
    dff = wdn_ref.shape[0]

    mix = None
    for g, ref in enumerate((oa_ref, ob_ref, oc_ref, od_ref)):
        part = jnp.dot(ref[...].astype(BF16), wo_ref[g * D_GRP:(g + 1) * D_GRP, :], preferred_element_type=F32)
        mix = part if mix is None else mix + part
    x1 = x_ref[...] + mix
    hf = _rms(x1, gffn_ref[...]).astype(BF16)

    def conv_cols(c0):
        up = jnp.dot(hf, wup_ref[:, c0:c0 + FF_CHUNK], preferred_element_type=F32)
        slabs = [fbuf_ref[r, :, c0:c0 + FF_CHUNK] for r in range(FFN_CONV_W - 1)]
        slabs += [up[t * ns:(t + 1) * ns] for t in range(ts)]
        for r in range(FFN_CONV_W - 1):
            tail_ref[r, :, c0:c0 + FF_CHUNK] = slabs[ts + r]
        outs = []
        for t in range(ts):
            y = cb_ref[:, c0:c0 + FF_CHUNK]
            for tap in range(FFN_CONV_W):
                y = y + cw_ref[tap:tap + 1, c0:c0 + FF_CHUNK] * slabs[t + tap]
            outs.append(y)
        return jnp.concatenate(outs, axis=0)

    acc = None
    for c in range(dff // FF_CHUNK):
        gate = conv_cols(c * FF_CHUNK)
        lin = conv_cols(dff + c * FF_CHUNK)
        act = (gate * jax.nn.sigmoid(gate) * lin).astype(BF16)
        down = jnp.dot(act, wdn_ref[c * FF_CHUNK:(c + 1) * FF_CHUNK, :], preferred_element_type=F32)
        acc = down if acc is None else acc + down
    y_ref[...] = x1 + acc


def _ffn_sample(x_tm, oa, ob, oc, od, fbuf_tm, lp, consts, l):
    n, D = x_tm.shape
    dff = lp["w_down"].shape[1]
    ns = n // consts["ts"]
    layer = lambda *s: pl.BlockSpec((None,) + s, lambda i: (l,) + (0,) * len(s), pipeline_mode=pl.Buffered(1))
    const = lambda *s: pl.BlockSpec(s, lambda i: (0,) * len(s))
    return pl.pallas_call(
        functools.partial(_ffn_sample_kernel, ts=consts["ts"]),
        grid=(1,),
        in_specs=[const(n, D)] + [const(n, D_GRP)] * 4 + [layer(D, D), layer(1, D), layer(D, 2 * dff),
                  layer(8, 2 * dff), layer(1, 2 * dff), layer(dff, D), const(2, ns, 2 * dff)],
        out_specs=[const(n, D), const(2, ns, 2 * dff)],
        out_shape=[jax.ShapeDtypeStruct((n, D), F32), jax.ShapeDtypeStruct((2, ns, 2 * dff), F32)],
        compiler_params=pltpu.CompilerParams(dimension_semantics=("arbitrary",), vmem_limit_bytes=VMEM_LIMIT),
        name="ffn_sample",
    )(x_tm, oa, ob, oc, od, lp["w_out"], lp["norm_ffn"], lp["w_up"], lp["ffn_conv_w"], lp["ffn_conv_b"],
      lp["w_down"], fbuf_tm)


def _pad_rows(a, rows):
    return jnp.pad(a, ((0, 0), (0, rows - a.shape[1]), (0, 0)))


def _rope_tables(pos):
    half = HEAD_DIM // 2
    inv = ROPE_THETA ** (-jnp.arange(half, dtype=F32) / half)
    ang = pos.astype(F32)[:, None] * inv[None, :]
    cos = jnp.tile(jnp.cos(ang), (1, 2 * N_HEADS))
    sin = jnp.sin(ang)
    sin_signed = jnp.tile(jnp.concatenate([-sin, sin], axis=1), (1, N_HEADS))
    return cos, sin_signed


def _prepare(norm_mix, w_in, sgu_norm, sgu_w, sgu_b, moba_qn, moba_kn, fox_qn, fox_kn, fox_fb, conv_w, conv_b,
             conv_ln_g, conv_ln_b, w_out, norm_ffn, w_up, ffn_conv_w, ffn_conv_b, w_down, ts):
    depth = w_in.shape[0]
    n_main = 8 * D_GRP
    w_in_p = jnp.concatenate([w_in[:, :, :n_main], w_in[:, :, n_main + N_HEADS:], w_in[:, :, n_main:n_main + N_HEADS],
                              jnp.zeros(w_in.shape[:2] + (LANES - N_HEADS,), w_in.dtype)], axis=-1).astype(BF16)
    per_head = lambda g: jnp.tile(g, (1, N_HEADS))
    gains = jnp.stack([sgu_norm.reshape(depth, D_GRP), per_head(moba_qn), per_head(moba_kn), per_head(fox_qn),
                       per_head(fox_kn)], axis=1)
    lanes_of_head = lambda a: jnp.repeat(a, HEAD_DIM, axis=-1)
    return {
        "norm_mix": norm_mix[:, None, :],
        "w_in": w_in_p,
        "gains": _pad_rows(gains, 8),
        "fb": jnp.pad(fox_fb, ((0, 0), (0, LANES - N_HEADS)))[:, None, :],
        "sgu_w": sgu_w,
        "sgu_bmix": lanes_of_head(jnp.swapaxes(sgu_b, 1, 2)),
        "sgu_w_s": lanes_of_head(jnp.transpose(sgu_w[:, :, :ts, :ts], (0, 2, 3, 1))),
        "sgu_b_s": lanes_of_head(jnp.swapaxes(sgu_b[:, :, :ts], 1, 2)),
        "conv_w": _pad_rows(conv_w, 32),
        "conv_b": conv_b[:, None, :],
        "conv_ln_g": conv_ln_g[:, None, :],
        "conv_ln_b": conv_ln_b[:, None, :],
        "w_out": w_out.astype(BF16),
        "norm_ffn": norm_ffn[:, None, :],
        "w_up": w_up.astype(BF16),
        "ffn_conv_w": _pad_rows(ffn_conv_w, 8),
        "ffn_conv_b": ffn_conv_b[:, None, :],
        "w_down": w_down.astype(BF16),
    }


def _constants(T, ts, ns, past_len, n_pages):
    blk = np.arange(D_GRP) // HEAD_DIM
    bd = jnp.asarray((blk[:, None] == blk[None, :]).astype(np.float32) / HEAD_DIM, dtype=BF16)
    cos_p, sin_p = _rope_tables(jnp.arange(T, dtype=jnp.int32))
    pos_s = past_len + jnp.repeat(jnp.arange(ts, dtype=jnp.int32), ns)
    cos_s, sin_s = _rope_tables(pos_s)
    rows = np.arange(n_pages * 8 * N_HEADS)
    src = (rows // (8 * N_HEADS)) * N_HEADS + (rows % (8 * N_HEADS)) // 8
    expand = jnp.asarray((src[:, None] == np.arange(n_pages * N_HEADS)[None, :]).astype(np.float32))
    return {"bd": bd, "cos_p": cos_p, "sin_p": sin_p, "cos_s": cos_s, "sin_s": sin_s, "expand": expand, "ts": ts}


def kernel(x_prompt, x_sample, cache_moba_k, cache_moba_v, cache_fox_k, cache_fox_v, cache_fox_logf, state_conv, state_ffn_conv, page_table, norm_mix, w_in, sgu_norm, sgu_w, sgu_b, moba_qn, moba_kn, fox_qn, fox_kn, fox_fb, conv_w, conv_b, conv_ln_g, conv_ln_b, w_out, norm_ffn, w_up, ffn_conv_w, ffn_conv_b, w_down):
    B, T, D = x_prompt.shape
    ns, ts, _ = x_sample.shape
    depth = w_in.shape[0]
    n_pages = page_table.shape[1]
    page = cache_moba_k.shape[2]
    past_len = n_pages * page
    assert T % ATT_BLOCK == 0 and T % TOKEN_TILE == 0 and T % FFN_TILE == 0
    assert past_len % MOBA_BLOCK == 0 and ts <= 8

    lp = _prepare(norm_mix, w_in, sgu_norm, sgu_w, sgu_b, moba_qn, moba_kn, fox_qn, fox_kn, fox_fb, conv_w, conv_b,
                  conv_ln_g, conv_ln_b, w_out, norm_ffn, w_up, ffn_conv_w, ffn_conv_b, w_down, ts)
    consts = _constants(T, ts, ns, past_len, n_pages)

    chan_major = lambda c: jnp.transpose(c, (0, 1, 3, 4, 2)).reshape(c.shape[0], c.shape[1], D_GRP, c.shape[2])
    pools = (chan_major(cache_moba_k), chan_major(cache_moba_v), chan_major(cache_fox_k), chan_major(cache_fox_v),
             jnp.transpose(cache_fox_logf, (0, 1, 3, 2)))
    hist_tm = jnp.transpose(state_conv, (0, 2, 1, 3))
    fbuf_tm = jnp.transpose(state_ffn_conv, (0, 2, 1, 3))

    xp = x_prompt
    xs = jnp.transpose(x_sample, (1, 0, 2)).reshape(ts * ns, D)
    P = {k: [] for k in ("conv", "ffn")}
    S = {k: [] for k in ("kb", "vb", "kc", "vc", "lf", "conv", "ffn", "sv")}
    carried = [jnp.zeros((depth, B, D_GRP, T), F32) for _ in range(4)] + [jnp.zeros((depth, B, N_HEADS, T), F32)]
    for l in range(depth):
        carried, (oa, glu, qbT, qcT16, kb16, kc16, vbT16, vcT16, km) = _inproj_prompt(xp, lp, consts, l, carried)
        ob, oc = _attn_prompt(qbT, qcT16, kb16, vbT16, kc16, vcT16, km, carried[4], l)
        od, conv_tail = _conv_prompt(glu, lp, l)
        xp, ffn_tail = _ffn_prompt(xp, oa, ob, oc, od, lp, l)
        P["conv"].append(conv_tail)
        P["ffn"].append(ffn_tail)

        soa, sod, sqb, skb, svb, sqc, skc, svc, slf, ssv, s_tail = _inproj_sample(xs, hist_tm, lp, consts, l)
        sob, soc = _attn_sample(page_table, sqb, skb, svb, sqc, skc, svc, slf, pools, consts, l)
        xs, sffn_tail = _ffn_sample(xs, soa, sob, soc, sod, fbuf_tm[l], lp, consts, l)
        for k, v in zip(("kb", "vb", "kc", "vc", "lf", "conv", "ffn", "sv"),
                        (skb, svb, skc, svc, slf, s_tail, sffn_tail, ssv)):
            S[k].append(v)

    P = {k: jnp.stack(v) for k, v in P.items()}
    P.update(zip(("kb", "vb", "kc", "vc", "lf"), carried))
    S = {k: jnp.stack(v) for k, v in S.items()}
    heads_p = lambda a: jnp.transpose(a.reshape(depth, B, N_HEADS, HEAD_DIM, T), (0, 1, 4, 2, 3))
    heads_s = lambda a: jnp.transpose(a.reshape(depth, ts, ns, N_HEADS, HEAD_DIM), (0, 2, 1, 3, 4))
    return (xp, jnp.transpose(xs.reshape(ts, ns, D), (1, 0, 2)),
            heads_p(P["kb"]), heads_p(P["vb"]), heads_p(P["kc"]), heads_p(P["vc"]),
            jnp.transpose(P["lf"], (0, 1, 3, 2)), P["conv"], P["ffn"],
            heads_s(S["kb"]), heads_s(S["vb"]), heads_s(S["kc"]), heads_s(S["vc"]),
            jnp.transpose(S["lf"].reshape(depth, ts, ns, LANES)[..., :N_HEADS], (0, 2, 1, 3)),
            jnp.transpose(S["conv"], (0, 2, 1, 3)), jnp.transpose(S["ffn"], (0, 2, 1, 3)),
            jnp.transpose(S["sv"].reshape(depth, ts, ns, D_GRP), (0, 2, 1, 3)))
```

```python
import functools

import numpy as np
import jax
import jax.numpy as jnp
from jax import lax
from jax.experimental import pallas as pl
from jax.experimental.pallas import tpu as pltpu

F32 = jnp.float32
BF16 = jnp.bfloat16
HIGHEST = lax.Precision.HIGHEST

HEAD_DIM = 64
N_HEADS = 4
D_GRP = N_HEADS * HEAD_DIM
SGU_CHUNK = 128
MOBA_BLOCK = 256
MOBA_TOPK = 3
CONV_W = 31
FFN_CONV_W = 3
ROPE_THETA = 10000.0
EPS = 1e-6
NEG_INF = -1e30
LANES = 128
Q_SCALE = HEAD_DIM ** -0.5
LOG2E = 1.4426950408889634
ACC_ROWS = HEAD_DIM + 16

TOKEN_TILE = 512
FFN_TILE = 512
ATT_BLOCK = 512
FF_CHUNK = 256
CONV_ROWS = 64
SAMPLE_SEQS = 2
VMEM_LIMIT = 52 * 1024 * 1024


def _gelu(x):
    return 0.5 * x * (1.0 + jnp.tanh(0.7978845608028654 * (x + 0.044715 * (x * x * x))))


def _log_sigmoid(x):
    return jnp.minimum(x, 0.0) - jnp.log1p(jnp.exp(-jnp.abs(x)))


def _rms(x, gain):
    ms = jnp.mean(x * x, axis=-1, keepdims=True)
    return x * lax.rsqrt(ms + EPS) * gain


def _head_rms(y, gain, bd):
    sq = y * y
    hi = sq.astype(BF16)
    lo = (sq - hi.astype(F32)).astype(BF16)
    ms = jnp.dot(hi, bd, preferred_element_type=F32) + jnp.dot(lo, bd, preferred_element_type=F32)
    return y * lax.rsqrt(ms + EPS) * gain


def _rope(y, cos, sin_signed):
    lane = lax.broadcasted_iota(jnp.int32, y.shape, 1)
    first = (lane % HEAD_DIM) < (HEAD_DIM // 2)
    partner = jnp.where(first, pltpu.roll(y, D_GRP - HEAD_DIM // 2, 1), pltpu.roll(y, HEAD_DIM // 2, 1))
    return y * cos + partner * sin_signed


def _head_of_lane(shape):
    return lax.broadcasted_iota(jnp.int32, shape, len(shape) - 1) // HEAD_DIM


def _in_groups(h, w_ref, bd, gains, fb, cos, sin_s):
    def grp(g, width=D_GRP):
        return jnp.dot(h, w_ref[:, g * D_GRP:g * D_GRP + width], preferred_element_type=F32)

    u = _gelu(grp(0))
    sv = _head_rms(_gelu(grp(1)), gains[0:1], bd)
    qb = _rope(_head_rms(grp(2), gains[1:2], bd), cos, sin_s) * Q_SCALE
    kb = _rope(_head_rms(grp(3), gains[2:3], bd), cos, sin_s)
    vb = grp(4)
    qc = _head_rms(grp(5), gains[3:4], bd) * Q_SCALE
    kc = _head_rms(grp(6), gains[4:5], bd)
    vc = grp(7)
    glu = grp(8) * jax.nn.sigmoid(grp(9))
    logf = _log_sigmoid(grp(10, LANES) + fb)
    return u, sv, qb, kb, vb, qc, kc, vc, logf, glu


def _layer_norm_silu(y, g, b):
    mu = jnp.mean(y, axis=-1, keepdims=True)
    yc = y - mu
    var = jnp.mean(yc * yc, axis=-1, keepdims=True)
    z = yc * lax.rsqrt(var + EPS) * g + b
    return z * jax.nn.sigmoid(z)


def _inproj_prompt_kernel(x_ref, gmix_ref, w_ref, bd_ref, gains_ref, fb_ref, cos_ref, sin_ref, wmix_ref, bmix_ref,
                          *refs, n_carried):
    (kbT_ref, vbT_ref, kcT_ref, vcT_ref, lfT_ref, oa_ref, glu_ref,
     qbT_ref, qcT16_ref, kb16_ref, kc16_ref, vbT16_ref, vcT16_ref, km_ref) = refs[n_carried:]
    h = _rms(x_ref[...], gmix_ref[...]).astype(BF16)
    u, sv, qb, kb, vb, qc, kc, vc, logf, glu = _in_groups(
        h, w_ref, bd_ref[...], gains_ref[...], fb_ref[...], cos_ref[...], sin_ref[...])
    glu_ref[...] = glu
    qbT_ref[...] = (qb * LOG2E).T
    qcT16_ref[...] = (qc * LOG2E).T.astype(BF16)
    kb16_ref[...] = kb.astype(BF16)
    kc16_ref[...] = kc.astype(BF16)
    kbT_ref[...] = kb.T
    kcT_ref[...] = kc.T
    for src, full_ref, half_ref in ((vb, vbT_ref, vbT16_ref), (vc, vcT_ref, vcT16_ref)):
        srcT = src.T
        full_ref[...] = srcT
        half_ref[...] = srcT.astype(BF16)
    lfT_ref[...] = logf.T[0:N_HEADS, :]
    for blk in range(km_ref.shape[0]):
        mean = jnp.mean(kb[blk * MOBA_BLOCK:(blk + 1) * MOBA_BLOCK], axis=0, keepdims=True)
        km_ref[blk] = jnp.broadcast_to(mean, (8, D_GRP))

    n = SGU_CHUNK
    row = lax.broadcasted_iota(jnp.int32, (n, n), 0)
    col = lax.broadcasted_iota(jnp.int32, (n, n), 1)
    head = _head_of_lane((n, D_GRP))
    svb = sv.astype(BF16)
    for c in range(x_ref.shape[0] // n):
        svc = svb[c * n:(c + 1) * n]
        mixed = bmix_ref[...]
        for g in range(N_HEADS):
            wg = jnp.where(row >= col, wmix_ref[g], 0.0).astype(BF16)
            mixed = mixed + jnp.where(head == g, jnp.dot(wg, svc, preferred_element_type=F32), 0.0)
        oa_ref[c * n:(c + 1) * n, :] = u[c * n:(c + 1) * n] * mixed


def _inproj_prompt(x, lp, consts, l, carried):
    B, T, D = x.shape
    depth = lp["w_in"].shape[0]
    tm = TOKEN_TILE
    nt = T // tm
    din = lp["w_in"].shape[-1]
    tok = lambda w: pl.BlockSpec((None, tm, w), lambda b, i: (b, i, 0))
    chan = lambda r: pl.BlockSpec((None, r, tm), lambda b, i: (b, 0, i))
    chan_l = lambda r: pl.BlockSpec((None, None, r, tm), lambda b, i: (l, b, 0, i))
    layer = lambda *s: pl.BlockSpec((None,) + s, lambda b, i: (l,) + (0,) * len(s))
    const = lambda *s: pl.BlockSpec(s, lambda b, i: (0,) * len(s))
    carried = tuple(carried)
    n_in = 10
    outs = pl.pallas_call(
        functools.partial(_inproj_prompt_kernel, n_carried=len(carried)),
        grid=(B, nt),
        in_specs=[tok(D), layer(1, D), layer(D, din), const(D_GRP, D_GRP), layer(8, D_GRP), layer(1, LANES),
                  pl.BlockSpec((tm, D_GRP), lambda b, i: (i, 0)), pl.BlockSpec((tm, D_GRP), lambda b, i: (i, 0)),
                  layer(N_HEADS, SGU_CHUNK, SGU_CHUNK), layer(SGU_CHUNK, D_GRP)]
                 + [pl.BlockSpec(memory_space=pl.ANY)] * len(carried),
        out_specs=[chan_l(D_GRP), chan_l(D_GRP), chan_l(D_GRP), chan_l(D_GRP), chan_l(N_HEADS), tok(D_GRP), tok(D_GRP),
                   chan(D_GRP), chan(D_GRP), tok(D_GRP), tok(D_GRP), chan(D_GRP), chan(D_GRP),
                   pl.BlockSpec((None, tm // MOBA_BLOCK, 8, D_GRP), lambda b, i: (b, i, 0, 0))],
        out_shape=[jax.ShapeDtypeStruct((depth, B, D_GRP, T), F32)] * 4
                  + [jax.ShapeDtypeStruct((depth, B, N_HEADS, T), F32),
                     jax.ShapeDtypeStruct((B, T, D_GRP), F32), jax.ShapeDtypeStruct((B, T, D_GRP), F32),
                     jax.ShapeDtypeStruct((B, D_GRP, T), F32), jax.ShapeDtypeStruct((B, D_GRP, T), BF16),
                     jax.ShapeDtypeStruct((B, T, D_GRP), BF16), jax.ShapeDtypeStruct((B, T, D_GRP), BF16),
                     jax.ShapeDtypeStruct((B, D_GRP, T), BF16), jax.ShapeDtypeStruct((B, D_GRP, T), BF16),
                     jax.ShapeDtypeStruct((B, T // MOBA_BLOCK, 8, D_GRP), F32)],
        input_output_aliases={n_in + k: k for k in range(len(carried))},
        compiler_params=pltpu.CompilerParams(dimension_semantics=("arbitrary", "arbitrary"),
                                             vmem_limit_bytes=VMEM_LIMIT),
        name="inproj_prompt",
    )(x, lp["norm_mix"], lp["w_in"], consts["bd"], lp["gains"], lp["fb"], consts["cos_p"], consts["sin_p"],
      lp["sgu_w"], lp["sgu_bmix"], *carried)
    return outs[:5], outs[5:]


def _pair_tables(nq):
    qi, kj = [], []
    for i in range(nq):
        qi.append(i)
        kj.append(i)
        for j in range(i):
            qi.append(i)
            kj.append(j)
    return np.asarray(qi, np.int32), np.asarray(kj, np.int32)


def _attn_prompt_kernel(qi_ref, kj_ref, qbT_ref, qcT_ref, kb_ref, vbT_ref, kc_ref, vcT_ref, km_ref, lfT_ref,
                        ob_ref, oc_ref, m_scr, acc_scr, qm_scr, crow_scr, ccol_scr, bias_scr, cmask_scr):
    s_idx = pl.program_id(1)
    i = qi_ref[s_idx]
    j = kj_ref[s_idx]
    tq = qbT_ref.shape[1]
    tk = kb_ref.shape[0]
    nblk = km_ref.shape[0]
    nsub = tk // MOBA_BLOCK
    diag = j == i
    last = jnp.logical_or(j == i - 1, i == 0)

    @pl.when(s_idx == 0)
    def _():
        crow_scr[...] = jnp.zeros_like(crow_scr)
        r = lax.broadcasted_iota(jnp.int32, (LANES, LANES), 0)
        c = lax.broadcasted_iota(jnp.int32, (LANES, LANES), 1)
        upper = jnp.where(r <= c, 1.0, 0.0).astype(F32)
        carry = jnp.zeros((N_HEADS, 1), F32)
        per_blk = tk // LANES
        for cc in range(lfT_ref.shape[1] // LANES):
            w = jnp.dot(lfT_ref[:, cc * LANES:(cc + 1) * LANES], upper, precision=HIGHEST,
                        preferred_element_type=F32) + carry
            crow_scr[cc // per_blk, 0:N_HEADS, (cc % per_blk) * LANES:(cc % per_blk + 1) * LANES] = w
            carry = w[:, LANES - 1:LANES]
        for blk in range(crow_scr.shape[0]):
            for h in range(N_HEADS):
                ccol_scr[blk, h] = jnp.broadcast_to(crow_scr[blk, h:h + 1, :] * LOG2E, (LANES, tk)).T
        cmask_scr[...] = jnp.where(lax.broadcasted_iota(jnp.int32, (tk, tq), 0)
                                   <= lax.broadcasted_iota(jnp.int32, (tk, tq), 1), 0.0, NEG_INF)

    def attend(a, k_ref, vT_ref, mask):
        s = jnp.dot(k_ref[...], qm_scr[a], preferred_element_type=F32)
        s = mask(s, a)
        m_old = m_scr[a]
        m_new = jnp.maximum(m_old, jnp.max(s, axis=0, keepdims=True))
        alpha = jnp.exp2(m_old - m_new)
        p16 = jnp.exp2(s - m_new).astype(BF16)
        m_scr[a] = m_new
        ones = jnp.ones((ACC_ROWS - HEAD_DIM, tk), BF16)
        for h in range(N_HEADS):
            slot = N_HEADS * a + h
            v1 = jnp.concatenate([vT_ref[h * HEAD_DIM:(h + 1) * HEAD_DIM, :], ones], axis=0)
            pv = jnp.dot(v1, p16[:, h * tq:(h + 1) * tq], preferred_element_type=F32)
            acc_scr[slot] = acc_scr[slot] * alpha[:, h * tq:(h + 1) * tq] + pv

    def both(mask):
        attend(0, kb_ref, vbT_ref, mask)
        attend(1, kc_ref, vcT_ref, mask)

    def minus_c(s):
        return s - jnp.concatenate([ccol_scr[j, h] for h in range(N_HEADS) for _ in range(tq // LANES)], axis=1)

    def plus_block_bias(s):
        return jnp.concatenate([s[k * MOBA_BLOCK:(k + 1) * MOBA_BLOCK] + bias_scr[pl.ds(nsub * j + k, 1), :]
                                for k in range(nsub)], axis=0)

    @pl.when(diag)
    def _():
        m_scr[...] = jnp.full_like(m_scr, NEG_INF)
        acc_scr[...] = jnp.zeros_like(acc_scr)
        qbT = qbT_ref[...]
        qbT16 = qbT.astype(BF16)
        qcT16 = qcT_ref[...]
        head_r = lax.broadcasted_iota(jnp.int32, (D_GRP, tq), 0) // HEAD_DIM
        for h in range(N_HEADS):
            qm_scr[0, :, h * tq:(h + 1) * tq] = jnp.where(head_r == h, qbT16, jnp.zeros_like(qbT16))
            qm_scr[1, :, h * tq:(h + 1) * tq] = jnp.where(head_r == h, qcT16, jnp.zeros_like(qcT16))

        sub = lax.broadcasted_iota(jnp.int32, (8, D_GRP), 0)
        km = jnp.zeros((8, D_GRP), F32)
        for n in range(nblk):
            km = jnp.where(sub == n, km_ref[n], km)
        head_l = _head_of_lane((8, D_GRP))
        blk = lax.broadcasted_iota(jnp.int32, (8, tq), 0)
        own = nsub * i + lax.broadcasted_iota(jnp.int32, (1, tq), 1) // MOBA_BLOCK
        for h in range(N_HEADS):
            gate = jnp.dot(jnp.where(head_l == h, km, 0.0), qbT, precision=HIGHEST,
                           preferred_element_type=F32)
            rank = jnp.zeros((8, tq), F32)
            for mblk in range(nblk - 1):
                gm = gate[mblk:mblk + 1, :]
                ahead = jnp.where(blk > mblk, jnp.where(gm >= gate, 1.0, 0.0), jnp.where(gm > gate, 1.0, 0.0))
                rank = rank + ahead * jnp.where(mblk < own, 1.0, 0.0)
            keep = ((rank < MOBA_TOPK) & (blk < own)) | (blk == own)
            bias_scr[:, h * tq:(h + 1) * tq] = jnp.where(keep, 0.0, NEG_INF)

        causal = jnp.concatenate([cmask_scr[...]] * N_HEADS, axis=1)

        def mask(s, a):
            s = minus_c(s) if a == 1 else plus_block_bias(s)
            return s + causal

        both(mask)

    @pl.when(jnp.logical_not(diag))
    def _():
        def mask(s, a):
            return minus_c(s) if a == 1 else plus_block_bias(s)

        both(mask)

    @pl.when(last)
    def _():
        for a, o_ref in enumerate((ob_ref, oc_ref)):
            heads = []
            for h in range(N_HEADS):
                acc = acc_scr[N_HEADS * a + h]
                heads.append(acc[0:HEAD_DIM] / acc[HEAD_DIM:HEAD_DIM + 1])
            o_ref[...] = jnp.concatenate(heads, axis=0).T


def _attn_prompt(qbT, qcT16, kb16, vbT16, kc16, vcT16, km, lfT_all, l):
    B, _, T = qbT.shape
    tq = ATT_BLOCK
    nq = T // tq
    nblk = T // MOBA_BLOCK
    assert nblk <= 8 and tq % MOBA_BLOCK == 0
    qi, kj = _pair_tables(nq)
    qspec = pl.BlockSpec((None, D_GRP, tq), lambda b, s, qi, kj: (b, 0, qi[s]))
    kspec = pl.BlockSpec((None, tq, D_GRP), lambda b, s, qi, kj: (b, kj[s], 0))
    vspec = pl.BlockSpec((None, D_GRP, tq), lambda b, s, qi, kj: (b, 0, kj[s]))
    grid_spec = pltpu.PrefetchScalarGridSpec(
        num_scalar_prefetch=2,
        grid=(B, len(qi)),
        in_specs=[qspec, qspec, kspec, vspec, kspec, vspec,
                  pl.BlockSpec((None, nblk, 8, D_GRP), lambda b, s, qi, kj: (b, 0, 0, 0)),
                  pl.BlockSpec((None, None, N_HEADS, T), lambda b, s, qi, kj: (l, b, 0, 0))],
        out_specs=[pl.BlockSpec((None, tq, D_GRP), lambda b, s, qi, kj: (b, qi[s], 0))] * 2,
        scratch_shapes=[pltpu.VMEM((2, 1, N_HEADS * tq), F32),
                        pltpu.VMEM((2 * N_HEADS, ACC_ROWS, tq), F32), pltpu.VMEM((2, D_GRP, N_HEADS * tq), BF16),
                        pltpu.VMEM((nq, 8, tq), F32), pltpu.VMEM((nq, N_HEADS, tq, LANES), F32),
                        pltpu.VMEM((8, N_HEADS * tq), F32), pltpu.VMEM((tq, tq), F32)],
    )
    return pl.pallas_call(
        _attn_prompt_kernel,
        grid_spec=grid_spec,
        out_shape=[jax.ShapeDtypeStruct((B, T, D_GRP), F32)] * 2,
        compiler_params=pltpu.CompilerParams(dimension_semantics=("arbitrary", "arbitrary"),
                                             vmem_limit_bytes=VMEM_LIMIT),
        name="attn_prompt",
    )(jnp.asarray(qi), jnp.asarray(kj), qbT, qcT16, kb16, vbT16, kc16, vcT16, km, lfT_all)


def _conv_prompt_kernel(x_ref, w_ref, b_ref, g_ref, beta_ref, o_ref, tail_ref, s_scr):
    T = x_ref.shape[0]
    pad = 32
    base = pad - (CONV_W - 1)
    s_scr[0, 0:pad, :] = jnp.zeros((pad, D_GRP), F32)
    s_scr[0, pad:pad + T, :] = x_ref[...]
    tail_ref[...] = s_scr[0, pad + T - (CONV_W - 1):pad + T, :]
    n_rows = T + pad - 8
    piece = n_rows // 7
    assert piece * 7 == n_rows and piece % 8 == 0
    for r in range(1, 8):
        for a in range(0, n_rows, piece):
            s_scr[r, a:a + piece, :] = s_scr[0, a + r:a + r + piece, :]
    w = w_ref[...]
    R = CONV_ROWS

    def chunk(c, _):
        r0 = pl.multiple_of(c * R, R)
        acc = jnp.zeros((R, D_GRP), F32) + b_ref[...]
        for tap in range(CONV_W):
            off = base + tap
            acc = acc + w[tap:tap + 1, :] * s_scr[off % 8, pl.ds(pl.multiple_of(r0 + off - off % 8, 8), R), :]
        o_ref[pl.ds(r0, R), :] = _layer_norm_silu(acc, g_ref[...], beta_ref[...])
        return 0

    lax.fori_loop(0, T // R, chunk, 0)


def _conv_prompt(glu, lp, l):
    B, T, _ = glu.shape
    layer = lambda *s: pl.BlockSpec((None,) + s, lambda b: (l,) + (0,) * len(s))
    return pl.pallas_call(
        _conv_prompt_kernel,
        grid=(B,),
        in_specs=[pl.BlockSpec((None, T, D_GRP), lambda b: (b, 0, 0)), layer(32, D_GRP), layer(1, D_GRP),
                  layer(1, D_GRP), layer(1, D_GRP)],
        out_specs=[pl.BlockSpec((None, T, D_GRP), lambda b: (b, 0, 0)),
                   pl.BlockSpec((None, CONV_W - 1, D_GRP), lambda b: (b, 0, 0))],
        out_shape=[jax.ShapeDtypeStruct((B, T, D_GRP), F32), jax.ShapeDtypeStruct((B, CONV_W - 1, D_GRP), F32)],
        scratch_shapes=[pltpu.VMEM((8, T + 32, D_GRP), F32)],
        compiler_params=pltpu.CompilerParams(dimension_semantics=("arbitrary",), vmem_limit_bytes=VMEM_LIMIT),
        name="conv_prompt",
    )(glu, lp["conv_w"], lp["conv_b"], lp["conv_ln_g"], lp["conv_ln_b"])


def _ffn_prompt_kernel(x_ref, oa_ref, ob_ref, oc_ref, od_ref, wo_ref, gffn_ref, wup_ref, cw_ref, cb_ref, wdn_ref,
                       y_ref, tail_ref, s_scr):
    t_idx = pl.program_id(1)
    tm = x_ref.shape[0]
    dff = wdn_ref.shape[0]
    nchunk = dff // FF_CHUNK

    @pl.when(t_idx == 0)
    def _():
        for k in range(2 * nchunk):
            s_scr[k, 0:8, :] = jnp.zeros((8, FF_CHUNK), F32)

    mix = None
    for g, ref in enumerate((oa_ref, ob_ref, oc_ref, od_ref)):
        part = jnp.dot(ref[...].astype(BF16), wo_ref[g * D_GRP:(g + 1) * D_GRP, :], preferred_element_type=F32)
        mix = part if mix is None else mix + part
    x1 = x_ref[...] + mix
    hf = _rms(x1, gffn_ref[...]).astype(BF16)

    def conv_cols(k, c0):
        up = jnp.dot(hf, wup_ref[:, c0:c0 + FF_CHUNK], preferred_element_type=F32)
        s_scr[k, 8:8 + tm, :] = up
        y = (cw_ref[0:1, c0:c0 + FF_CHUNK] * s_scr[k, 6:6 + tm, :]
             + cw_ref[1:2, c0:c0 + FF_CHUNK] * s_scr[k, 7:7 + tm, :]
             + cw_ref[2:3, c0:c0 + FF_CHUNK] * up + cb_ref[:, c0:c0 + FF_CHUNK])
        last2 = s_scr[k, tm + 6:tm + 8, :]
        s_scr[k, 6:8, :] = last2
        tail_ref[:, c0:c0 + FF_CHUNK] = last2
        return y

    acc = None
    for c in range(nchunk):
        gate = conv_cols(2 * c, c * FF_CHUNK)
        lin = conv_cols(2 * c + 1, dff + c * FF_CHUNK)
        act = (gate * jax.nn.sigmoid(gate) * lin).astype(BF16)
        down = jnp.dot(act, wdn_ref[c * FF_CHUNK:(c + 1) * FF_CHUNK, :], preferred_element_type=F32)
        acc = down if acc is None else acc + down
    y_ref[...] = x1 + acc


def _ffn_prompt(x, oa, ob, oc, od, lp, l):
    B, T, D = x.shape
    tm = FFN_TILE
    dff = lp["w_down"].shape[1]
    tok = lambda w: pl.BlockSpec((None, tm, w), lambda b, i: (b, i, 0))
    layer = lambda *s: pl.BlockSpec((None,) + s, lambda b, i: (l,) + (0,) * len(s))
    once =lambda *s: pl.BlockSpec((None,) + s, lambda b, i: (l,) + (0,) * len(s), pipeline_mode=pl.Buffered(1))
    return pl.pallas_call(
        _ffn_prompt_kernel,
        grid=(B, T // tm),
        in_specs=[tok(D), tok(D_GRP), tok(D_GRP), tok(D_GRP), tok(D_GRP), once(D, D), layer(1, D),
                  once(D, 2 * dff), layer(8, 2 * dff), layer(1, 2 * dff), once(dff, D)],
        out_specs=[tok(D), pl.BlockSpec((None, 2, 2 * dff), lambda b, i: (b, 0, 0))],
        out_shape=[jax.ShapeDtypeStruct((B, T, D), F32), jax.ShapeDtypeStruct((B, 2, 2 * dff), F32)],
        scratch_shapes=[pltpu.VMEM((2 * dff // FF_CHUNK, tm + 8, FF_CHUNK), F32)],
        compiler_params=pltpu.CompilerParams(dimension_semantics=("arbitrary", "arbitrary"),
                                             vmem_limit_bytes=VMEM_LIMIT),
        name="ffn_prompt",
    )(x, oa, ob, oc, od, lp["w_out"], lp["norm_ffn"], lp["w_up"], lp["ffn_conv_w"], lp["ffn_conv_b"], lp["w_down"])


def _inproj_sample_kernel(x_ref, gmix_ref, w_ref, bd_ref, gains_ref, fb_ref, cos_ref, sin_ref, wmix_ref, bmix_ref,
                          hist_ref, cw_ref, cb_ref, g_ref, beta_ref,
                          oa_ref, od_ref, qb_ref, kb_ref, vb_ref, qc_ref, kc_ref, vc_ref, lf_ref, sv_ref, tail_ref):
    ts = wmix_ref.shape[0]
    ns = x_ref.shape[0] // ts
    h = _rms(x_ref[...], gmix_ref[...]).astype(BF16)
    u, sv, qb, kb, vb, qc, kc, vc, logf, glu = _in_groups(
        h, w_ref, bd_ref[...], gains_ref[...], fb_ref[...], cos_ref[...], sin_ref[...])
    qb_ref[...] = qb
    kb_ref[...] = kb
    vb_ref[...] = vb
    qc_ref[...] = qc
    kc_ref[...] = kc
    vc_ref[...] = vc
    lf_ref[...] = logf
    sv_ref[...] = sv

    for t in range(ts):
        mixed = bmix_ref[t:t + 1, :]
        for s in range(t + 1):
            mixed = mixed + wmix_ref[t, s:s + 1, :] * sv[s * ns:(s + 1) * ns]
        oa_ref[t * ns:(t + 1) * ns, :] = u[t * ns:(t + 1) * ns] * mixed

    nh = hist_ref.shape[0]
    slabs = [hist_ref[r] for r in range(nh)] + [glu[t * ns:(t + 1) * ns] for t in range(ts)]
    for r in range(nh):
        tail_ref[r] = slabs[ts + r]
    for t in range(ts):
        acc = jnp.zeros((ns, D_GRP), F32) + cb_ref[...]
        for tap in range(CONV_W):
            acc = acc + cw_ref[tap:tap + 1, :] * slabs[t + tap]
        od_ref[t * ns:(t + 1) * ns, :] = _layer_norm_silu(acc, g_ref[...], beta_ref[...])


def _inproj_sample(x_tm, hist_tm, lp, consts, l):
    n, D = x_tm.shape
    ts, ns = consts["ts"], n // consts["ts"]
    din = lp["w_in"].shape[-1]
    layer = lambda *s: pl.BlockSpec((None,) + s, lambda i: (l,) + (0,) * len(s))
    const = lambda *s: pl.BlockSpec(s, lambda i: (0,) * len(s))
    tok = jax.ShapeDtypeStruct((n, D_GRP), F32)
    nh = CONV_W - 1
    return pl.pallas_call(
        _inproj_sample_kernel,
        grid=(1,),
        in_specs=[const(n, D), layer(1, D), layer(D, din), const(D_GRP, D_GRP), layer(8, D_GRP), layer(1, LANES),
                  const(n, D_GRP), const(n, D_GRP), layer(ts, ts, D_GRP), layer(ts, D_GRP),
                  layer(nh, ns, D_GRP), layer(32, D_GRP), layer(1, D_GRP), layer(1, D_GRP), layer(1, D_GRP)],
        out_specs=[const(n, D_GRP)] * 8 + [const(n, LANES), const(n, D_GRP), const(nh, ns, D_GRP)],
        out_shape=[tok] * 8 + [jax.ShapeDtypeStruct((n, LANES), F32), tok,
                               jax.ShapeDtypeStruct((nh, ns, D_GRP), F32)],
        compiler_params=pltpu.CompilerParams(dimension_semantics=("arbitrary",), vmem_limit_bytes=VMEM_LIMIT),
        name="inproj_sample",
    )(x_tm, lp["norm_mix"], lp["w_in"], consts["bd"], lp["gains"], lp["fb"], consts["cos_s"], consts["sin_s"],
      lp["sgu_w_s"], lp["sgu_b_s"], hist_tm, lp["conv_w"], lp["conv_b"], lp["conv_ln_g"], lp["conv_ln_b"])


def _attn_sample_kernel(pt_ref, qb_ref, kb_ref, vb_ref, qc_ref, kc_ref, vc_ref, lf_ref,
                        pkb_ref, pvb_ref, pkc_ref, pvc_ref, plf_ref,
                        ob_ref, oc_ref,
                        kb_buf, vb_buf, kc_buf, vc_buf, lf_buf, sem, *, layer, n_pages):
    g = pl.program_id(0)
    ng = pl.num_programs(0)
    page = pkb_ref.shape[3]
    pools = (pkb_ref, pvb_ref, pkc_ref, pvc_ref)
    bufs = (kb_buf, vb_buf, kc_buf, vc_buf)

    def copies(step, parity):
        out = []
        for k in range(SAMPLE_SEQS):
            seq = step * SAMPLE_SEQS + k
            slot = SAMPLE_SEQS * parity + k
            for p in range(n_pages):
                phys = pt_ref[seq * n_pages + p]
                for a in range(4):
                    out.append(pltpu.make_async_copy(pools[a].at[layer, phys],
                                                     bufs[a].at[slot, :, pl.ds(p * page, page)], sem.at[slot, a]))
                out.append(pltpu.make_async_copy(plf_ref.at[layer, phys],
                                                 lf_buf.at[slot, pl.ds(p * N_HEADS, N_HEADS), :], sem.at[slot, 4]))
        return out

    parity = g % 2

    @pl.when(g == 0)
    def _():
        for cp in copies(0, 0):
            cp.start()

    @pl.when(g + 1 < ng)
    def _():
        for cp in copies(g + 1, 1 - parity):
            cp.start()

    for cp in copies(g, parity):
        cp.wait()

    for k in range(SAMPLE_SEQS):
        _attn_sample_one(g * SAMPLE_SEQS + k, SAMPLE_SEQS * parity + k, qb_ref, kb_ref, vb_ref, qc_ref, kc_ref, vc_ref,
                         lf_ref, ob_ref, oc_ref, kb_buf, vb_buf, kc_buf, vc_buf, lf_buf,
                         page=page, n_pages=n_pages)


def _attn_sample_one(b, slot, qb_ref, kb_ref, vb_ref, qc_ref, kc_ref, vc_ref, lf_ref, ob_ref, oc_ref,
                     kb_buf, vb_buf, kc_buf, vc_buf, lf_buf, *, page, n_pages):
    ts = qb_ref.shape[0]
    past = n_pages * page
    rows = 8 * N_HEADS
    head_l = _head_of_lane((8, D_GRP))
    sub8 = lax.broadcasted_iota(jnp.int32, (8, D_GRP), 0)

    def rows8(ref, width=D_GRP):
        sub = lax.broadcasted_iota(jnp.int32, (8, width), 0)
        out = jnp.zeros((8, width), F32)
        for t in range(ts):
            out = jnp.where(sub == t, ref[t, pl.ds(b, 1), :], out)
        return out

    def q_rows(ref):
        q8 = rows8(ref)
        return jnp.concatenate([jnp.where(head_l == h, q8, 0.0) for h in range(N_HEADS)], axis=0)

    r_sub = lax.broadcasted_iota(jnp.int32, (rows, 8), 0) % 8
    c_new = lax.broadcasted_iota(jnp.int32, (rows, 8), 1)
    new_ok = (c_new <= r_sub) & (c_new < ts)
    new_ok = new_ok | ((r_sub >= ts) & (c_new == 0))

    def attend(vT, v_new, s_past, s_new):
        m = jnp.maximum(jnp.max(s_past, axis=1, keepdims=True), jnp.max(s_new, axis=1, keepdims=True))
        p_past = jnp.exp(s_past - m)
        p_new = jnp.exp(s_new - m)
        den = jnp.sum(p_past, axis=1, keepdims=True) + jnp.sum(p_new, axis=1, keepdims=True)
        o = lax.dot_general(p_past.astype(BF16), vT, (((1,), (1,)), ((), ())), preferred_element_type=F32)
        o = o + jnp.dot(p_new.astype(BF16), v_new.astype(BF16), preferred_element_type=F32)
        o = o / den
        out8 = jnp.zeros((8, D_GRP), F32)
        for h in range(N_HEADS):
            out8 = jnp.where(head_l == h, o[8 * h:8 * h + 8], out8)
        return out8

    def write(o_ref, out8):
        for t in range(ts):
            o_ref[t, pl.ds(b, 1), :] = out8[t:t + 1]

    qr = q_rows(qb_ref)
    kTf = kb_buf[slot]
    n_blk = past // MOBA_BLOCK
    lane_k = lax.broadcasted_iota(jnp.int32, (D_GRP, LANES), 1)
    km = jnp.zeros((D_GRP, LANES), F32)
    for nblk in range(n_blk):
        km = jnp.where(lane_k == nblk,
                       jnp.mean(kTf[:, nblk * MOBA_BLOCK:(nblk + 1) * MOBA_BLOCK], axis=1, keepdims=True), km)
    gate = jnp.dot(qr, km, precision=HIGHEST, preferred_element_type=F32)
    lane_g = lax.broadcasted_iota(jnp.int32, (rows, LANES), 1)
    rank = jnp.zeros((rows, LANES), F32)
    for mblk in range(n_blk):
        gm = gate[:, mblk:mblk + 1]
        rank = rank + jnp.where((gm > gate) | ((gm == gate) & (mblk < lane_g)), 1.0, 0.0)
    sel = jnp.where((rank < MOBA_TOPK) & (lane_g < n_blk), 1.0, 0.0)
    qrb = qr.astype(BF16)
    s_full = jnp.dot(qrb, kTf.astype(BF16), preferred_element_type=F32)
    s_past = jnp.concatenate(
        [jnp.where(sel[:, nblk:nblk + 1] > 0.5, s_full[:, nblk * MOBA_BLOCK:(nblk + 1) * MOBA_BLOCK], NEG_INF)
         for nblk in range(n_blk)], axis=1)
    k_new = rows8(kb_ref)
    v_new = rows8(vb_ref)
    s_new = lax.dot_general(qrb, k_new.astype(BF16), (((1,), (1,)), ((), ())), preferred_element_type=F32)
    s_new = jnp.where(new_ok, s_new, NEG_INF)
    write(ob_ref, attend(vb_buf[slot].astype(BF16), v_new, s_past, s_new))

    x = lf_buf[slot]
    r = lax.broadcasted_iota(jnp.int32, (page, page), 0)
    c = lax.broadcasted_iota(jnp.int32, (page, page), 1)
    upper = jnp.where(r <= c, 1.0, 0.0).astype(F32)
    within = jnp.dot(x, upper, precision=HIGHEST, preferred_element_type=F32)
    np4 = n_pages * N_HEADS
    rr = lax.broadcasted_iota(jnp.int32, (np4, np4), 0)
    cc = lax.broadcasted_iota(jnp.int32, (np4, np4), 1)
    same_head = (rr % N_HEADS) == (cc % N_HEADS)
    tot = jnp.broadcast_to(within[:, page - 1:page], (np4, page))
    before = jnp.dot(jnp.where(same_head & (cc < rr - rr % N_HEADS), 1.0, 0.0).astype(F32), tot,
                     precision=HIGHEST, preferred_element_type=F32)
    total = jnp.dot(jnp.where(same_head, 1.0, 0.0).astype(F32), tot, precision=HIGHEST, preferred_element_type=F32)
    after = total - (within + before)

    def page_bias(p):
        return jnp.concatenate([jnp.broadcast_to(after[N_HEADS * p + h:N_HEADS * p + h + 1, :], (8, page))
                                for h in range(N_HEADS)], axis=0)

    lf8 = rows8(lf_ref, LANES)
    cum_rows = []
    run = jnp.zeros((1, LANES), F32)
    for t in range(ts):
        run = run + lf8[t:t + 1]
        cum_rows.append(run)
    cnew_cols = jnp.zeros((rows, 8), F32)
    for t in range(ts):
        col_t = jnp.concatenate([jnp.broadcast_to(cum_rows[t][:, h:h + 1], (8, 1)) for h in range(N_HEADS)],
                                axis=0)
        cnew_cols = jnp.where(c_new == t, col_t, cnew_cols)
    qr = q_rows(qc_ref)
    qrb = qr.astype(BF16)
    kTf = kc_buf[slot]
    s_full = jnp.dot(qrb, kTf.astype(BF16), preferred_element_type=F32)
    s_past = jnp.concatenate([s_full[:, p * page:(p + 1) * page] + page_bias(p) for p in range(n_pages)], axis=1)
    k_new = rows8(kc_ref)
    v_new = rows8(vc_ref)
    s_new = lax.dot_general(qrb, k_new.astype(BF16), (((1,), (1,)), ((), ())), preferred_element_type=F32)
    s_new = jnp.where(new_ok, s_new - cnew_cols, NEG_INF)
    write(oc_ref, attend(vc_buf[slot].astype(BF16), v_new, s_past, s_new))


def _attn_sample(page_table, qb, kb, vb, qc, kc, vc, lf, pools, consts, l):
    ns, n_pages = page_table.shape
    ts = consts["ts"]
    pkb, pvb, pkc, pvc, plf = pools
    page = pkb.shape[3]
    past = n_pages * page
    tm3 = lambda a: a.reshape(ts, ns, a.shape[-1])
    whole = lambda w: pl.BlockSpec((ts, ns, w), lambda b, pt: (0, 0, 0))
    anyspec = pl.BlockSpec(memory_space=pl.ANY)
    grid_spec = pltpu.PrefetchScalarGridSpec(
        num_scalar_prefetch=1,
        grid=(ns // SAMPLE_SEQS,),
        in_specs=[whole(D_GRP)] * 6 + [whole(LANES)] + [anyspec] * 5,
        out_specs=[whole(D_GRP), whole(D_GRP)],
        scratch_shapes=[pltpu.VMEM((2 * SAMPLE_SEQS, D_GRP, past), F32)] * 4
                       + [pltpu.VMEM((2 * SAMPLE_SEQS, n_pages * N_HEADS, page), F32),
                          pltpu.SemaphoreType.DMA((2 * SAMPLE_SEQS, 5))],
    )
    ob, oc = pl.pallas_call(
        functools.partial(_attn_sample_kernel, layer=l, n_pages=n_pages),
        grid_spec=grid_spec,
        out_shape=[jax.ShapeDtypeStruct((ts, ns, D_GRP), F32)] * 2,
        compiler_params=pltpu.CompilerParams(dimension_semantics=("arbitrary",), vmem_limit_bytes=VMEM_LIMIT),
        name="attn_sample",
    )(page_table.reshape(-1), tm3(qb), tm3(kb), tm3(vb), tm3(qc), tm3(kc), tm3(vc), tm3(lf),
      pkb, pvb, pkc, pvc, plf)
    return ob.reshape(ts * ns, D_GRP), oc.reshape(ts * ns, D_GRP)


def _ffn_sample_kernel(x_ref, oa_ref, ob_ref, oc_ref, od_ref, wo_ref, gffn_ref, wup_ref, cw_ref, cb_ref, wdn_ref,
                       fbuf_ref, y_ref, tail_ref, *, ts):
    n = x_ref.shape[0]
    ns = n // ts
    dff = wdn_ref.shape[0]

    mix = None
    for g, ref in enumerate((oa_ref, ob_ref, oc_ref, od_ref)):
        part = jnp.dot(ref[...].astype(BF16), wo_ref[g * D_GRP:(g + 1) * D_GRP, :], preferred_element_type=F32)
        mix = part if mix is None else mix + part
    x1 = x_ref[...] + mix
    hf = _rms(x1, gffn_ref[...]).astype(BF16)

    def conv_cols(c0):
        up = jnp.dot(hf, wup_ref[:, c0:c0 + FF_CHUNK], preferred_element_type=F32)
        slabs = [fbuf_ref[r, :, c0:c0 + FF_CHUNK] for r in range(FFN_CONV_W - 1)]
        slabs += [up[t * ns:(t + 1) * ns] for t in range(ts)]
        for r in range(FFN_CONV_W - 1):
            tail_ref[r, :, c0:c0 + FF_CHUNK] = slabs[ts + r]
        outs = []
        for t in range(ts):
            y = cb_ref[:, c0:c0 + FF_CHUNK]
            for tap in range(FFN_CONV_W):
                y = y + cw_ref[tap:tap + 1, c0:c0 + FF_CHUNK] * slabs[t + tap]
            outs.append(y)
        return jnp.concatenate(outs, axis=0)

    acc = None
    for c in range(dff // FF_CHUNK):
        gate = conv_cols(c * FF_CHUNK)
        lin = conv_cols(dff + c * FF_CHUNK)
        act = (gate * jax.nn.sigmoid(gate) * lin).astype(BF16)
        down = jnp.dot(act, wdn_ref[c * FF_CHUNK:(c + 1) * FF_CHUNK, :], preferred_element_type=F32)
        acc = down if acc is None else acc + down
    y_ref[...] = x1 + acc


def _ffn_sample(x_tm, oa, ob, oc, od, fbuf_tm, lp, consts, l):
    n, D = x_tm.shape
    dff = lp["w_down"].shape[1]
    ns = n // consts["ts"]
    layer = lambda *s: pl.BlockSpec((None,) + s, lambda i: (l,) + (0,) * len(s), pipeline_mode=pl.Buffered(1))
    const = lambda *s: pl.BlockSpec(s, lambda i: (0,) * len(s))
    return pl.pallas_call(
        functools.partial(_ffn_sample_kernel, ts=consts["ts"]),
        grid=(1,),
        in_specs=[const(n, D)] + [const(n, D_GRP)] * 4 + [layer(D, D), layer(1, D), layer(D, 2 * dff),
                  layer(8, 2 * dff), layer(1, 2 * dff), layer(dff, D), const(2, ns, 2 * dff)],
        out_specs=[const(n, D), const(2, ns, 2 * dff)],
        out_shape=[jax.ShapeDtypeStruct((n, D), F32), jax.ShapeDtypeStruct((2, ns, 2 * dff), F32)],
        compiler_params=pltpu.CompilerParams(dimension_semantics=("arbitrary",), vmem_limit_bytes=VMEM_LIMIT),
        name="ffn_sample",
    )(x_tm, oa, ob, oc, od, lp["w_out"], lp["norm_ffn"], lp["w_up"], lp["ffn_conv_w"], lp["ffn_conv_b"],
      lp["w_down"], fbuf_tm)


def _pad_rows(a, rows):
    return jnp.pad(a, ((0, 0), (0, rows - a.shape[1]), (0, 0)))


def _rope_tables(pos):
    half = HEAD_DIM // 2
    inv = ROPE_THETA ** (-jnp.arange(half, dtype=F32) / half)
    ang = pos.astype(F32)[:, None] * inv[None, :]
    cos = jnp.tile(jnp.cos(ang), (1, 2 * N_HEADS))
    sin = jnp.sin(ang)
    sin_signed = jnp.tile(jnp.concatenate([-sin, sin], axis=1), (1, N_HEADS))
    return cos, sin_signed


def _prepare(norm_mix, w_in, sgu_norm, sgu_w, sgu_b, moba_qn, moba_kn, fox_qn, fox_kn, fox_fb, conv_w, conv_b,
             conv_ln_g, conv_ln_b, w_out, norm_ffn, w_up, ffn_conv_w, ffn_conv_b, w_down, ts):
    depth = w_in.shape[0]
    n_main = 8 * D_GRP
    w16 = w_in.astype(BF16)
    w_in_p = jnp.concatenate([w16[:, :, :n_main], w16[:, :, n_main + N_HEADS:], w16[:, :, n_main:n_main + N_HEADS],
                              jnp.zeros(w_in.shape[:2] + (LANES - N_HEADS,), BF16)], axis=-1)
    per_head = lambda g: jnp.tile(g, (1, N_HEADS))
    gains = jnp.stack([sgu_norm.reshape(depth, D_GRP), per_head(moba_qn), per_head(moba_kn), per_head(fox_qn),
                       per_head(fox_kn)], axis=1)
    lanes_of_head = lambda a: jnp.repeat(a, HEAD_DIM, axis=-1)
    return {
        "norm_mix": norm_mix[:, None, :],
        "w_in": w_in_p,
        "gains": _pad_rows(gains, 8),
        "fb": jnp.pad(fox_fb, ((0, 0), (0, LANES - N_HEADS)))[:, None, :],
        "sgu_w": sgu_w,
        "sgu_bmix": lanes_of_head(jnp.swapaxes(sgu_b, 1, 2)),
        "sgu_w_s": lanes_of_head(jnp.transpose(sgu_w[:, :, :ts, :ts], (0, 2, 3, 1))),
        "sgu_b_s": lanes_of_head(jnp.swapaxes(sgu_b[:, :, :ts], 1, 2)),
        "conv_w": _pad_rows(conv_w, 32),
        "conv_b": conv_b[:, None, :],
        "conv_ln_g": conv_ln_g[:, None, :],
        "conv_ln_b": conv_ln_b[:, None, :],
        "w_out": w_out.astype(BF16),
        "norm_ffn": norm_ffn[:, None, :],
        "w_up": w_up.astype(BF16),
        "ffn_conv_w": _pad_rows(ffn_conv_w, 8),
        "ffn_conv_b": ffn_conv_b[:, None, :],
        "w_down": w_down.astype(BF16),
    }


def _constants(T, ts, ns, past_len, n_pages):
    blk = np.arange(D_GRP) // HEAD_DIM
    bd = jnp.asarray((blk[:, None] == blk[None, :]).astype(np.float32) / HEAD_DIM, dtype=BF16)
    cos_p, sin_p = _rope_tables(jnp.arange(T, dtype=jnp.int32))
    pos_s = past_len + jnp.repeat(jnp.arange(ts, dtype=jnp.int32), ns)
    cos_s, sin_s = _rope_tables(pos_s)
    return {"bd": bd, "cos_p": cos_p, "sin_p": sin_p, "cos_s": cos_s, "sin_s": sin_s, "ts": ts}


def kernel(x_prompt, x_sample, cache_moba_k, cache_moba_v, cache_fox_k, cache_fox_v, cache_fox_logf, state_conv, state_ffn_conv, page_table, norm_mix, w_in, sgu_norm, sgu_w, sgu_b, moba_qn, moba_kn, fox_qn, fox_kn, fox_fb, conv_w, conv_b, conv_ln_g, conv_ln_b, w_out, norm_ffn, w_up, ffn_conv_w, ffn_conv_b, w_down):
    B, T, D = x_prompt.shape
    ns, ts, _ = x_sample.shape
    depth = w_in.shape[0]
    n_pages = page_table.shape[1]
    page = cache_moba_k.shape[2]
    past_len = n_pages * page
    assert T % ATT_BLOCK == 0 and T % TOKEN_TILE == 0 and T % FFN_TILE == 0
    assert past_len % MOBA_BLOCK == 0 and ts <= 8 and ns % SAMPLE_SEQS == 0

    lp = _prepare(norm_mix, w_in, sgu_norm, sgu_w, sgu_b, moba_qn, moba_kn, fox_qn, fox_kn, fox_fb, conv_w, conv_b,
                  conv_ln_g, conv_ln_b, w_out, norm_ffn, w_up, ffn_conv_w, ffn_conv_b, w_down, ts)
    consts = _constants(T, ts, ns, past_len, n_pages)

    chan_major = lambda c: jnp.transpose(c, (0, 1, 3, 4, 2)).reshape(c.shape[0], c.shape[1], D_GRP, c.shape[2])
    pools = (chan_major(cache_moba_k), chan_major(cache_moba_v), chan_major(cache_fox_k), chan_major(cache_fox_v),
             jnp.transpose(cache_fox_logf, (0, 1, 3, 2)))
    hist_tm = jnp.transpose(state_conv, (0, 2, 1, 3))
    fbuf_tm = jnp.transpose(state_ffn_conv, (0, 2, 1, 3))

    xp = x_prompt
    xs = jnp.transpose(x_sample, (1, 0, 2)).reshape(ts * ns, D)
    P = {k: [] for k in ("conv", "ffn")}
    S = {k: [] for k in ("kb", "vb", "kc", "vc", "lf", "conv", "ffn", "sv")}
    carried = [jnp.zeros((depth, B, D_GRP, T), F32) for _ in range(4)] + [jnp.zeros((depth, B, N_HEADS, T), F32)]
    for l in range(depth):
        carried, (oa, glu, qbT, qcT16, kb16, kc16, vbT16, vcT16, km) = _inproj_prompt(xp, lp, consts, l, carried)
        ob, oc = _attn_prompt(qbT, qcT16, kb16, vbT16, kc16, vcT16, km, carried[4], l)
        od, conv_tail = _conv_prompt(glu, lp, l)
        xp, ffn_tail = _ffn_prompt(xp, oa, ob, oc, od, lp, l)
        P["conv"].append(conv_tail)
        P["ffn"].append(ffn_tail)

        soa, sod, sqb, skb, svb, sqc, skc, svc, slf, ssv, s_tail = _inproj_sample(xs, hist_tm, lp, consts, l)
        sob, soc = _attn_sample(page_table, sqb, skb, svb, sqc, skc, svc, slf, pools, consts, l)
        xs, sffn_tail = _ffn_sample(xs, soa, sob, soc, sod, fbuf_tm[l], lp, consts, l)
        for k, v in zip(("kb", "vb", "kc", "vc", "lf", "conv", "ffn", "sv"),
                        (skb, svb, skc, svc, slf, s_tail, sffn_tail, ssv)):
            S[k].append(v)

    P = {k: jnp.stack(v) for k, v in P.items()}
    P.update(zip(("kb", "vb", "kc", "vc", "lf"), carried))
    S = {k: jnp.stack(v) for k, v in S.items()}
    heads_p = lambda a: jnp.transpose(a.reshape(depth, B, N_HEADS, HEAD_DIM, T), (0, 1, 4, 2, 3))
    heads_s = lambda a: jnp.transpose(a.reshape(depth, ts, ns, N_HEADS, HEAD_DIM), (0, 2, 1, 3, 4))
    return (xp, jnp.transpose(xs.reshape(ts, ns, D), (1, 0, 2)),
            heads_p(P["kb"]), heads_p(P["vb"]), heads_p(P["kc"]), heads_p(P["vc"]),
            jnp.transpose(P["lf"], (0, 1, 3, 2)), P["conv"], P["ffn"],
            heads_s(S["kb"]), heads_s(S["vb"]), heads_s(S["kc"]), heads_s(S["vc"]),
            jnp.transpose(S["lf"].reshape(depth, ts, ns, LANES)[..., :N_HEADS], (0, 2, 1, 3)),
            jnp.transpose(S["conv"], (0, 2, 1, 3)), jnp.transpose(S["ffn"], (0, 2, 1, 3)),
            jnp.transpose(S["sv"].reshape(depth, ts, ns, D_GRP), (0, 2, 1, 3)))
```

```python
import functools

import numpy as np
import jax
import jax.numpy as jnp
from jax import lax
from jax.experimental import pallas as pl
from jax.experimental.pallas import tpu as pltpu

F32 = jnp.float32
BF16 = jnp.bfloat16
HIGHEST = lax.Precision.HIGHEST

HEAD_DIM = 64
N_HEADS = 4
D_GRP = N_HEADS * HEAD_DIM
SGU_CHUNK = 128
MOBA_BLOCK = 256
MOBA_TOPK = 3
CONV_W = 31
FFN_CONV_W = 3
ROPE_THETA = 10000.0
EPS = 1e-6
NEG_INF = -1e30
LANES = 128
Q_SCALE = HEAD_DIM ** -0.5
LOG2E = 1.4426950408889634
ACC_ROWS = HEAD_DIM + 16

TOKEN_TILE = 512
FFN_TILE = 512
ATT_BLOCK = 512
FF_CHUNK = 256
CONV_ROWS = 64
SAMPLE_SEQS = 2
VMEM_LIMIT = 52 * 1024 * 1024


def _gelu(x):
    return 0.5 * x * (1.0 + jnp.tanh(0.7978845608028654 * (x + 0.044715 * (x * x * x))))


def _log_sigmoid(x):
    return jnp.minimum(x, 0.0) - jnp.log1p(jnp.exp(-jnp.abs(x)))


def _rms(x, gain):
    ms = jnp.mean(x * x, axis=-1, keepdims=True)
    return x * lax.rsqrt(ms + EPS) * gain


def _head_rms(y, gain, bd):
    sq = y * y
    hi = sq.astype(BF16)
    lo = (sq - hi.astype(F32)).astype(BF16)
    ms = jnp.dot(hi, bd, preferred_element_type=F32) + jnp.dot(lo, bd, preferred_element_type=F32)
    return y * lax.rsqrt(ms + EPS) * gain


def _rope(y, cos, sin_signed):
    lane = lax.broadcasted_iota(jnp.int32, y.shape, 1)
    first = (lane % HEAD_DIM) < (HEAD_DIM // 2)
    partner = jnp.where(first, pltpu.roll(y, D_GRP - HEAD_DIM // 2, 1), pltpu.roll(y, HEAD_DIM // 2, 1))
    return y * cos + partner * sin_signed


def _head_of_lane(shape):
    return lax.broadcasted_iota(jnp.int32, shape, len(shape) - 1) // HEAD_DIM


def _in_groups(h, w_ref, bd, gains, fb, cos, sin_s):
    def grp(g, width=D_GRP):
        return jnp.dot(h, w_ref[:, g * D_GRP:g * D_GRP + width], preferred_element_type=F32)

    u = _gelu(grp(0))
    sv = _head_rms(_gelu(grp(1)), gains[0:1], bd)
    qb = _rope(_head_rms(grp(2), gains[1:2], bd), cos, sin_s) * Q_SCALE
    kb = _rope(_head_rms(grp(3), gains[2:3], bd), cos, sin_s)
    vb = grp(4)
    qc = _head_rms(grp(5), gains[3:4], bd) * Q_SCALE
    kc = _head_rms(grp(6), gains[4:5], bd)
    vc = grp(7)
    glu = grp(8) * jax.nn.sigmoid(grp(9))
    logf = _log_sigmoid(grp(10, LANES) + fb)
    return u, sv, qb, kb, vb, qc, kc, vc, logf, glu


def _layer_norm_silu(y, g, b):
    mu = jnp.mean(y, axis=-1, keepdims=True)
    yc = y - mu
    var = jnp.mean(yc * yc, axis=-1, keepdims=True)
    z = yc * lax.rsqrt(var + EPS) * g + b
    return z * jax.nn.sigmoid(z)


def _inproj_prompt_kernel(x_ref, gmix_ref, w_ref, bd_ref, gains_ref, fb_ref, cos_ref, sin_ref, wmix_ref, bmix_ref,
                          *refs, n_carried):
    (kbT_ref, vbT_ref, kcT_ref, vcT_ref, lfT_ref, oa_ref, glu_ref,
     qbT_ref, qcT16_ref, kb16_ref, kc16_ref, vbT16_ref, vcT16_ref, km_ref) = refs[n_carried:]
    h = _rms(x_ref[...], gmix_ref[...]).astype(BF16)
    u, sv, qb, kb, vb, qc, kc, vc, logf, glu = _in_groups(
        h, w_ref, bd_ref[...], gains_ref[...], fb_ref[...], cos_ref[...], sin_ref[...])
    glu_ref[...] = glu
    qbT_ref[...] = (qb * LOG2E).T
    qcT16_ref[...] = (qc * LOG2E).T.astype(BF16)
    kb16_ref[...] = kb.astype(BF16)
    kc16_ref[...] = kc.astype(BF16)
    kbT_ref[...] = kb.T
    kcT_ref[...] = kc.T
    for src, full_ref, half_ref in ((vb, vbT_ref, vbT16_ref), (vc, vcT_ref, vcT16_ref)):
        srcT = src.T
        full_ref[...] = srcT
        half_ref[...] = srcT.astype(BF16)
    lfT_ref[...] = logf.T[0:N_HEADS, :]
    for blk in range(km_ref.shape[0]):
        mean = jnp.mean(kb[blk * MOBA_BLOCK:(blk + 1) * MOBA_BLOCK], axis=0, keepdims=True)
        km_ref[blk] = jnp.broadcast_to(mean, (8, D_GRP))

    n = SGU_CHUNK
    row = lax.broadcasted_iota(jnp.int32, (n, n), 0)
    col = lax.broadcasted_iota(jnp.int32, (n, n), 1)
    head = _head_of_lane((n, D_GRP))
    svb = sv.astype(BF16)
    for c in range(x_ref.shape[0] // n):
        svc = svb[c * n:(c + 1) * n]
        mixed = bmix_ref[...]
        for g in range(N_HEADS):
            wg = jnp.where(row >= col, wmix_ref[g], 0.0).astype(BF16)
            mixed = mixed + jnp.where(head == g, jnp.dot(wg, svc, preferred_element_type=F32), 0.0)
        oa_ref[c * n:(c + 1) * n, :] = u[c * n:(c + 1) * n] * mixed


def _inproj_prompt(x, lp, consts, l, carried):
    B, T, D = x.shape
    depth = lp["w_in"].shape[0]
    tm = TOKEN_TILE
    nt = T // tm
    din = lp["w_in"].shape[-1]
    tok = lambda w: pl.BlockSpec((None, tm, w), lambda b, i: (b, i, 0))
    chan = lambda r: pl.BlockSpec((None, r, tm), lambda b, i: (b, 0, i))
    chan_l = lambda r: pl.BlockSpec((None, None, r, tm), lambda b, i: (l, b, 0, i))
    layer = lambda *s: pl.BlockSpec((None,) + s, lambda b, i: (l,) + (0,) * len(s))
    const = lambda *s: pl.BlockSpec(s, lambda b, i: (0,) * len(s))
    carried = tuple(carried)
    n_in = 10
    outs = pl.pallas_call(
        functools.partial(_inproj_prompt_kernel, n_carried=len(carried)),
        grid=(B, nt),
        in_specs=[tok(D), layer(1, D), layer(D, din), const(D_GRP, D_GRP), layer(8, D_GRP), layer(1, LANES),
                  pl.BlockSpec((tm, D_GRP), lambda b, i: (i, 0)), pl.BlockSpec((tm, D_GRP), lambda b, i: (i, 0)),
                  layer(N_HEADS, SGU_CHUNK, SGU_CHUNK), layer(SGU_CHUNK, D_GRP)]
                 + [pl.BlockSpec(memory_space=pl.ANY)] * len(carried),
        out_specs=[chan_l(D_GRP), chan_l(D_GRP), chan_l(D_GRP), chan_l(D_GRP), chan_l(N_HEADS), tok(D_GRP), tok(D_GRP),
                   chan(D_GRP), chan(D_GRP), tok(D_GRP), tok(D_GRP), chan(D_GRP), chan(D_GRP),
                   pl.BlockSpec((None, tm // MOBA_BLOCK, 8, D_GRP), lambda b, i: (b, i, 0, 0))],
        out_shape=[jax.ShapeDtypeStruct((depth, B, D_GRP, T), F32)] * 4
                  + [jax.ShapeDtypeStruct((depth, B, N_HEADS, T), F32),
                     jax.ShapeDtypeStruct((B, T, D_GRP), F32), jax.ShapeDtypeStruct((B, T, D_GRP), F32),
                     jax.ShapeDtypeStruct((B, D_GRP, T), F32), jax.ShapeDtypeStruct((B, D_GRP, T), BF16),
                     jax.ShapeDtypeStruct((B, T, D_GRP), BF16), jax.ShapeDtypeStruct((B, T, D_GRP), BF16),
                     jax.ShapeDtypeStruct((B, D_GRP, T), BF16), jax.ShapeDtypeStruct((B, D_GRP, T), BF16),
                     jax.ShapeDtypeStruct((B, T // MOBA_BLOCK, 8, D_GRP), F32)],
        input_output_aliases={n_in + k: k for k in range(len(carried))},
        compiler_params=pltpu.CompilerParams(dimension_semantics=("arbitrary", "arbitrary"),
                                             vmem_limit_bytes=VMEM_LIMIT),
        name="inproj_prompt",
    )(x, lp["norm_mix"], lp["w_in"], consts["bd"], lp["gains"], lp["fb"], consts["cos_p"], consts["sin_p"],
      lp["sgu_w"], lp["sgu_bmix"], *carried)
    return outs[:5], outs[5:]


def _pair_tables(nq):
    qi, kj = [], []
    for i in range(nq):
        qi.append(i)
        kj.append(i)
        for j in range(i):
            qi.append(i)
            kj.append(j)
    return np.asarray(qi, np.int32), np.asarray(kj, np.int32)


def _attn_prompt_kernel(qi_ref, kj_ref, qbT_ref, qcT_ref, kb_ref, vbT_ref, kc_ref, vcT_ref, km_ref, lfT_ref,
                        ob_ref, oc_ref, m_scr, acc_scr, qm_scr, crow_scr, ccol_scr, bias_scr, cmask_scr):
    s_idx = pl.program_id(1)
    i = qi_ref[s_idx]
    j = kj_ref[s_idx]
    tq = qbT_ref.shape[1]
    tk = kb_ref.shape[0]
    nblk = km_ref.shape[0]
    nsub = tk // MOBA_BLOCK
    diag = j == i
    last = jnp.logical_or(j == i - 1, i == 0)

    @pl.when(s_idx == 0)
    def _():
        crow_scr[...] = jnp.zeros_like(crow_scr)
        r = lax.broadcasted_iota(jnp.int32, (LANES, LANES), 0)
        c = lax.broadcasted_iota(jnp.int32, (LANES, LANES), 1)
        upper = jnp.where(r <= c, 1.0, 0.0).astype(F32)
        carry = jnp.zeros((N_HEADS, 1), F32)
        per_blk = tk // LANES
        for cc in range(lfT_ref.shape[1] // LANES):
            w = jnp.dot(lfT_ref[:, cc * LANES:(cc + 1) * LANES], upper, precision=HIGHEST,
                        preferred_element_type=F32) + carry
            crow_scr[cc // per_blk, 0:N_HEADS, (cc % per_blk) * LANES:(cc % per_blk + 1) * LANES] = w
            carry = w[:, LANES - 1:LANES]
        for blk in range(crow_scr.shape[0]):
            for h in range(N_HEADS):
                ccol_scr[blk, h] = jnp.broadcast_to(crow_scr[blk, h:h + 1, :] * LOG2E, (LANES, tk)).T
        cmask_scr[...] = jnp.where(lax.broadcasted_iota(jnp.int32, (tk, tq), 0)
                                   <= lax.broadcasted_iota(jnp.int32, (tk, tq), 1), 0.0, NEG_INF)

    def attend(a, k_ref, vT_ref, mask):
        s = jnp.dot(k_ref[...], qm_scr[a], preferred_element_type=F32)
        s = mask(s, a)
        m_old = m_scr[a]
        m_new = jnp.maximum(m_old, jnp.max(s, axis=0, keepdims=True))
        alpha = jnp.exp2(m_old - m_new)
        p16 = jnp.exp2(s - m_new).astype(BF16)
        m_scr[a] = m_new
        ones = jnp.ones((ACC_ROWS - HEAD_DIM, tk), BF16)
        for h in range(N_HEADS):
            slot = N_HEADS * a + h
            v1 = jnp.concatenate([vT_ref[h * HEAD_DIM:(h + 1) * HEAD_DIM, :], ones], axis=0)
            pv = jnp.dot(v1, p16[:, h * tq:(h + 1) * tq], preferred_element_type=F32)
            acc_scr[slot] = acc_scr[slot] * alpha[:, h * tq:(h + 1) * tq] + pv

    def both(mask):
        attend(0, kb_ref, vbT_ref, mask)
        attend(1, kc_ref, vcT_ref, mask)

    def minus_c(s):
        return s - jnp.concatenate([ccol_scr[j, h] for h in range(N_HEADS) for _ in range(tq // LANES)], axis=1)

    def plus_block_bias(s):
        return jnp.concatenate([s[k * MOBA_BLOCK:(k + 1) * MOBA_BLOCK] + bias_scr[pl.ds(nsub * j + k, 1), :]
                                for k in range(nsub)], axis=0)

    @pl.when(diag)
    def _():
        m_scr[...] = jnp.full_like(m_scr, NEG_INF)
        acc_scr[...] = jnp.zeros_like(acc_scr)
        qbT = qbT_ref[...]
        qbT16 = qbT.astype(BF16)
        qcT16 = qcT_ref[...]
        head_r = lax.broadcasted_iota(jnp.int32, (D_GRP, tq), 0) // HEAD_DIM
        for h in range(N_HEADS):
            qm_scr[0, :, h * tq:(h + 1) * tq] = jnp.where(head_r == h, qbT16, jnp.zeros_like(qbT16))
            qm_scr[1, :, h * tq:(h + 1) * tq] = jnp.where(head_r == h, qcT16, jnp.zeros_like(qcT16))

        sub = lax.broadcasted_iota(jnp.int32, (8, D_GRP), 0)
        km = jnp.zeros((8, D_GRP), F32)
        for n in range(nblk):
            km = jnp.where(sub == n, km_ref[n], km)
        head_l = _head_of_lane((8, D_GRP))
        blk = lax.broadcasted_iota(jnp.int32, (8, tq), 0)
        own = nsub * i + lax.broadcasted_iota(jnp.int32, (1, tq), 1) // MOBA_BLOCK
        for h in range(N_HEADS):
            gate = jnp.dot(jnp.where(head_l == h, km, 0.0), qbT, precision=HIGHEST,
                           preferred_element_type=F32)
            rank = jnp.zeros((8, tq), F32)
            for mblk in range(nblk - 1):
                gm = gate[mblk:mblk + 1, :]
                ahead = jnp.where(blk > mblk, jnp.where(gm >= gate, 1.0, 0.0), jnp.where(gm > gate, 1.0, 0.0))
                rank = rank + ahead * jnp.where(mblk < own, 1.0, 0.0)
            keep = ((rank < MOBA_TOPK) & (blk < own)) | (blk == own)
            bias_scr[:, h * tq:(h + 1) * tq] = jnp.where(keep, 0.0, NEG_INF)

        causal = jnp.concatenate([cmask_scr[...]] * N_HEADS, axis=1)

        def mask(s, a):
            s = minus_c(s) if a == 1 else plus_block_bias(s)
            return s + causal

        both(mask)

    @pl.when(jnp.logical_not(diag))
    def _():
        def mask(s, a):
            return minus_c(s) if a == 1 else plus_block_bias(s)

        both(mask)

    @pl.when(last)
    def _():
        for a, o_ref in enumerate((ob_ref, oc_ref)):
            heads = []
            for h in range(N_HEADS):
                acc = acc_scr[N_HEADS * a + h]
                heads.append(acc[0:HEAD_DIM] / acc[HEAD_DIM:HEAD_DIM + 1])
            o_ref[...] = jnp.concatenate(heads, axis=0).T


def _attn_prompt(qbT, qcT16, kb16, vbT16, kc16, vcT16, km, lfT_all, l):
    B, _, T = qbT.shape
    tq = ATT_BLOCK
    nq = T // tq
    nblk = T // MOBA_BLOCK
    assert nblk <= 8 and tq % MOBA_BLOCK == 0
    qi, kj = _pair_tables(nq)
    qspec = pl.BlockSpec((None, D_GRP, tq), lambda b, s, qi, kj: (b, 0, qi[s]))
    kspec = pl.BlockSpec((None, tq, D_GRP), lambda b, s, qi, kj: (b, kj[s], 0))
    vspec = pl.BlockSpec((None, D_GRP, tq), lambda b, s, qi, kj: (b, 0, kj[s]))
    grid_spec = pltpu.PrefetchScalarGridSpec(
        num_scalar_prefetch=2,
        grid=(B, len(qi)),
        in_specs=[qspec, qspec, kspec, vspec, kspec, vspec,
                  pl.BlockSpec((None, nblk, 8, D_GRP), lambda b, s, qi, kj: (b, 0, 0, 0)),
                  pl.BlockSpec((None, None, N_HEADS, T), lambda b, s, qi, kj: (l, b, 0, 0))],
        out_specs=[pl.BlockSpec((None, tq, D_GRP), lambda b, s, qi, kj: (b, qi[s], 0))] * 2,
        scratch_shapes=[pltpu.VMEM((2, 1, N_HEADS * tq), F32),
                        pltpu.VMEM((2 * N_HEADS, ACC_ROWS, tq), F32), pltpu.VMEM((2, D_GRP, N_HEADS * tq), BF16),
                        pltpu.VMEM((nq, 8, tq), F32), pltpu.VMEM((nq, N_HEADS, tq, LANES), F32),
                        pltpu.VMEM((8, N_HEADS * tq), F32), pltpu.VMEM((tq, tq), F32)],
    )
    return pl.pallas_call(
        _attn_prompt_kernel,
        grid_spec=grid_spec,
        out_shape=[jax.ShapeDtypeStruct((B, T, D_GRP), F32)] * 2,
        compiler_params=pltpu.CompilerParams(dimension_semantics=("arbitrary", "arbitrary"),
                                             vmem_limit_bytes=VMEM_LIMIT),
        name="attn_prompt",
    )(jnp.asarray(qi), jnp.asarray(kj), qbT, qcT16, kb16, vbT16, kc16, vcT16, km, lfT_all)


def _conv_prompt_kernel(x_ref, w_ref, b_ref, g_ref, beta_ref, o_ref, tail_ref, s_scr):
    T = x_ref.shape[0]
    pad = 32
    base = pad - (CONV_W - 1)
    s_scr[0, 0:pad, :] = jnp.zeros((pad, D_GRP), F32)
    s_scr[0, pad:pad + T, :] = x_ref[...]
    tail_ref[...] = s_scr[0, pad + T - (CONV_W - 1):pad + T, :]
    n_rows = T + pad - 8
    piece = n_rows // 7
    assert piece * 7 == n_rows and piece % 8 == 0
    for r in range(1, 8):
        for a in range(0, n_rows, piece):
            s_scr[r, a:a + piece, :] = s_scr[0, a + r:a + r + piece, :]
    w = w_ref[...]
    R = CONV_ROWS

    def chunk(c, _):
        r0 = pl.multiple_of(c * R, R)
        acc = jnp.zeros((R, D_GRP), F32) + b_ref[...]
        for tap in range(CONV_W):
            off = base + tap
            acc = acc + w[tap:tap + 1, :] * s_scr[off % 8, pl.ds(pl.multiple_of(r0 + off - off % 8, 8), R), :]
        o_ref[pl.ds(r0, R), :] = _layer_norm_silu(acc, g_ref[...], beta_ref[...])
        return 0

    lax.fori_loop(0, T // R, chunk, 0)


def _conv_prompt(glu, lp, l):
    B, T, _ = glu.shape
    layer = lambda *s: pl.BlockSpec((None,) + s, lambda b: (l,) + (0,) * len(s))
    return pl.pallas_call(
        _conv_prompt_kernel,
        grid=(B,),
        in_specs=[pl.BlockSpec((None, T, D_GRP), lambda b: (b, 0, 0)), layer(32, D_GRP), layer(1, D_GRP),
                  layer(1, D_GRP), layer(1, D_GRP)],
        out_specs=[pl.BlockSpec((None, T, D_GRP), lambda b: (b, 0, 0)),
                   pl.BlockSpec((None, CONV_W - 1, D_GRP), lambda b: (b, 0, 0))],
        out_shape=[jax.ShapeDtypeStruct((B, T, D_GRP), F32), jax.ShapeDtypeStruct((B, CONV_W - 1, D_GRP), F32)],
        scratch_shapes=[pltpu.VMEM((8, T + 32, D_GRP), F32)],
        compiler_params=pltpu.CompilerParams(dimension_semantics=("arbitrary",), vmem_limit_bytes=VMEM_LIMIT),
        name="conv_prompt",
    )(glu, lp["conv_w"], lp["conv_b"], lp["conv_ln_g"], lp["conv_ln_b"])


def _ffn_prompt_kernel(x_ref, oa_ref, ob_ref, oc_ref, od_ref, wo_ref, gffn_ref, wup_ref, cw_ref, cb_ref, wdn_ref,
                       y_ref, tail_ref, s_scr):
    t_idx = pl.program_id(1)
    tm = x_ref.shape[0]
    dff = wdn_ref.shape[0]
    nchunk = dff // FF_CHUNK

    @pl.when(t_idx == 0)
    def _():
        for k in range(2 * nchunk):
            s_scr[k, 0:8, :] = jnp.zeros((8, FF_CHUNK), F32)

    mix = None
    for g, ref in enumerate((oa_ref, ob_ref, oc_ref, od_ref)):
        part = jnp.dot(ref[...].astype(BF16), wo_ref[g * D_GRP:(g + 1) * D_GRP, :], preferred_element_type=F32)
        mix = part if mix is None else mix + part
    x1 = x_ref[...] + mix
    hf = _rms(x1, gffn_ref[...]).astype(BF16)

    def conv_cols(k, c0):
        up = jnp.dot(hf, wup_ref[:, c0:c0 + FF_CHUNK], preferred_element_type=F32)
        s_scr[k, 8:8 + tm, :] = up
        y = (cw_ref[0:1, c0:c0 + FF_CHUNK] * s_scr[k, 6:6 + tm, :]
             + cw_ref[1:2, c0:c0 + FF_CHUNK] * s_scr[k, 7:7 + tm, :]
             + cw_ref[2:3, c0:c0 + FF_CHUNK] * up + cb_ref[:, c0:c0 + FF_CHUNK])
        last2 = s_scr[k, tm + 6:tm + 8, :]
        s_scr[k, 6:8, :] = last2
        tail_ref[:, c0:c0 + FF_CHUNK] = last2
        return y

    acc = None
    ahead = (conv_cols(0, 0), conv_cols(1, dff))
    for c in range(nchunk):
        gate, lin = ahead
        if c + 1 < nchunk:
            ahead = (conv_cols(2 * c + 2, (c + 1) * FF_CHUNK), conv_cols(2 * c + 3, dff + (c + 1) * FF_CHUNK))
        act = (gate * jax.nn.sigmoid(gate) * lin).astype(BF16)
        down = jnp.dot(act, wdn_ref[c * FF_CHUNK:(c + 1) * FF_CHUNK, :], preferred_element_type=F32)
        acc = down if acc is None else acc + down
    y_ref[...] = x1 + acc


def _ffn_prompt(x, oa, ob, oc, od, lp, l):
    B, T, D = x.shape
    tm = FFN_TILE
    dff = lp["w_down"].shape[1]
    tok = lambda w: pl.BlockSpec((None, tm, w), lambda b, i: (b, i, 0))
    layer = lambda *s: pl.BlockSpec((None,) + s, lambda b, i: (l,) + (0,) * len(s))
    once =lambda *s: pl.BlockSpec((None,) + s, lambda b, i: (l,) + (0,) * len(s), pipeline_mode=pl.Buffered(1))
    return pl.pallas_call(
        _ffn_prompt_kernel,
        grid=(B, T // tm),
        in_specs=[tok(D), tok(D_GRP), tok(D_GRP), tok(D_GRP), tok(D_GRP), once(D, D), layer(1, D),
                  once(D, 2 * dff), layer(8, 2 * dff), layer(1, 2 * dff), once(dff, D)],
        out_specs=[tok(D), pl.BlockSpec((None, 2, 2 * dff), lambda b, i: (b, 0, 0))],
        out_shape=[jax.ShapeDtypeStruct((B, T, D), F32), jax.ShapeDtypeStruct((B, 2, 2 * dff), F32)],
        scratch_shapes=[pltpu.VMEM((2 * dff // FF_CHUNK, tm + 8, FF_CHUNK), F32)],
        compiler_params=pltpu.CompilerParams(dimension_semantics=("arbitrary", "arbitrary"),
                                             vmem_limit_bytes=VMEM_LIMIT),
        name="ffn_prompt",
    )(x, oa, ob, oc, od, lp["w_out"], lp["norm_ffn"], lp["w_up"], lp["ffn_conv_w"], lp["ffn_conv_b"], lp["w_down"])


def _inproj_sample_kernel(x_ref, gmix_ref, w_ref, bd_ref, gains_ref, fb_ref, cos_ref, sin_ref, wmix_ref, bmix_ref,
                          hist_ref, cw_ref, cb_ref, g_ref, beta_ref,
                          oa_ref, od_ref, qb_ref, kb_ref, vb_ref, qc_ref, kc_ref, vc_ref, lf_ref, sv_ref, tail_ref):
    ts = wmix_ref.shape[0]
    ns = x_ref.shape[0] // ts
    h = _rms(x_ref[...], gmix_ref[...]).astype(BF16)
    u, sv, qb, kb, vb, qc, kc, vc, logf, glu = _in_groups(
        h, w_ref, bd_ref[...], gains_ref[...], fb_ref[...], cos_ref[...], sin_ref[...])
    qb_ref[...] = qb
    kb_ref[...] = kb
    vb_ref[...] = vb
    qc_ref[...] = qc
    kc_ref[...] = kc
    vc_ref[...] = vc
    lf_ref[...] = logf
    sv_ref[...] = sv

    for t in range(ts):
        mixed = bmix_ref[t:t + 1, :]
        for s in range(t + 1):
            mixed = mixed + wmix_ref[t, s:s + 1, :] * sv[s * ns:(s + 1) * ns]
        oa_ref[t * ns:(t + 1) * ns, :] = u[t * ns:(t + 1) * ns] * mixed

    nh = hist_ref.shape[0]
    slabs = [hist_ref[r] for r in range(nh)] + [glu[t * ns:(t + 1) * ns] for t in range(ts)]
    for r in range(nh):
        tail_ref[r] = slabs[ts + r]
    for t in range(ts):
        acc = jnp.zeros((ns, D_GRP), F32) + cb_ref[...]
        for tap in range(CONV_W):
            acc = acc + cw_ref[tap:tap + 1, :] * slabs[t + tap]
        od_ref[t * ns:(t + 1) * ns, :] = _layer_norm_silu(acc, g_ref[...], beta_ref[...])


def _inproj_sample(x_tm, hist_tm, lp, consts, l):
    n, D = x_tm.shape
    ts, ns = consts["ts"], n // consts["ts"]
    din = lp["w_in"].shape[-1]
    layer = lambda *s: pl.BlockSpec((None,) + s, lambda i: (l,) + (0,) * len(s))
    const = lambda *s: pl.BlockSpec(s, lambda i: (0,) * len(s))
    tok = jax.ShapeDtypeStruct((n, D_GRP), F32)
    nh = CONV_W - 1
    return pl.pallas_call(
        _inproj_sample_kernel,
        grid=(1,),
        in_specs=[const(n, D), layer(1, D), layer(D, din), const(D_GRP, D_GRP), layer(8, D_GRP), layer(1, LANES),
                  const(n, D_GRP), const(n, D_GRP), layer(ts, ts, D_GRP), layer(ts, D_GRP),
                  layer(nh, ns, D_GRP), layer(32, D_GRP), layer(1, D_GRP), layer(1, D_GRP), layer(1, D_GRP)],
        out_specs=[const(n, D_GRP)] * 8 + [const(n, LANES), const(n, D_GRP), const(nh, ns, D_GRP)],
        out_shape=[tok] * 8 + [jax.ShapeDtypeStruct((n, LANES), F32), tok,
                               jax.ShapeDtypeStruct((nh, ns, D_GRP), F32)],
        compiler_params=pltpu.CompilerParams(dimension_semantics=("arbitrary",), vmem_limit_bytes=VMEM_LIMIT),
        name="inproj_sample",
    )(x_tm, lp["norm_mix"], lp["w_in"], consts["bd"], lp["gains"], lp["fb"], consts["cos_s"], consts["sin_s"],
      lp["sgu_w_s"], lp["sgu_b_s"], hist_tm, lp["conv_w"], lp["conv_b"], lp["conv_ln_g"], lp["conv_ln_b"])


def _attn_sample_kernel(pt_ref, qb_ref, kb_ref, vb_ref, qc_ref, kc_ref, vc_ref, lf_ref,
                        pkb_ref, pvb_ref, pkc_ref, pvc_ref, plf_ref,
                        ob_ref, oc_ref,
                        kb_buf, vb_buf, kc_buf, vc_buf, lf_buf, sem, *, layer, n_pages):
    g = pl.program_id(0)
    ng = pl.num_programs(0)
    page = pkb_ref.shape[3]
    pools = (pkb_ref, pvb_ref, pkc_ref, pvc_ref)
    bufs = (kb_buf, vb_buf, kc_buf, vc_buf)

    def copies(step, parity):
        out = []
        for k in range(SAMPLE_SEQS):
            seq = step * SAMPLE_SEQS + k
            slot = SAMPLE_SEQS * parity + k
            for p in range(n_pages):
                phys = pt_ref[seq * n_pages + p]
                for a in range(4):
                    out.append(pltpu.make_async_copy(pools[a].at[layer, phys],
                                                     bufs[a].at[slot, :, pl.ds(p * page, page)], sem.at[slot, a]))
                out.append(pltpu.make_async_copy(plf_ref.at[layer, phys],
                                                 lf_buf.at[slot, pl.ds(p * N_HEADS, N_HEADS), :], sem.at[slot, 4]))
        return out

    parity = g % 2

    @pl.when(g == 0)
    def _():
        for cp in copies(0, 0):
            cp.start()

    @pl.when(g + 1 < ng)
    def _():
        for cp in copies(g + 1, 1 - parity):
            cp.start()

    for cp in copies(g, parity):
        cp.wait()

    for k in range(SAMPLE_SEQS):
        _attn_sample_one(g * SAMPLE_SEQS + k, SAMPLE_SEQS * parity + k, qb_ref, kb_ref, vb_ref, qc_ref, kc_ref, vc_ref,
                         lf_ref, ob_ref, oc_ref, kb_buf, vb_buf, kc_buf, vc_buf, lf_buf,
                         page=page, n_pages=n_pages)


def _attn_sample_one(b, slot, qb_ref, kb_ref, vb_ref, qc_ref, kc_ref, vc_ref, lf_ref, ob_ref, oc_ref,
                     kb_buf, vb_buf, kc_buf, vc_buf, lf_buf, *, page, n_pages):
    ts = qb_ref.shape[0]
    past = n_pages * page
    rows = 8 * N_HEADS
    head_l = _head_of_lane((8, D_GRP))
    sub8 = lax.broadcasted_iota(jnp.int32, (8, D_GRP), 0)

    def rows8(ref, width=D_GRP):
        sub = lax.broadcasted_iota(jnp.int32, (8, width), 0)
        out = jnp.zeros((8, width), F32)
        for t in range(ts):
            out = jnp.where(sub == t, ref[t, pl.ds(b, 1), :], out)
        return out

    def q_rows(ref):
        q8 = rows8(ref)
        return jnp.concatenate([jnp.where(head_l == h, q8, 0.0) for h in range(N_HEADS)], axis=0)

    r_sub = lax.broadcasted_iota(jnp.int32, (rows, 8), 0) % 8
    c_new = lax.broadcasted_iota(jnp.int32, (rows, 8), 1)
    new_ok = (c_new <= r_sub) & (c_new < ts)
    new_ok = new_ok | ((r_sub >= ts) & (c_new == 0))

    def attend(vT, v_new, s_past, s_new):
        m = jnp.maximum(jnp.max(s_past, axis=1, keepdims=True), jnp.max(s_new, axis=1, keepdims=True))
        p_past = jnp.exp(s_past - m)
        p_new = jnp.exp(s_new - m)
        den = jnp.sum(p_past, axis=1, keepdims=True) + jnp.sum(p_new, axis=1, keepdims=True)
        o = lax.dot_general(p_past.astype(BF16), vT, (((1,), (1,)), ((), ())), preferred_element_type=F32)
        o = o + jnp.dot(p_new.astype(BF16), v_new.astype(BF16), preferred_element_type=F32)
        o = o / den
        out8 = jnp.zeros((8, D_GRP), F32)
        for h in range(N_HEADS):
            out8 = jnp.where(head_l == h, o[8 * h:8 * h + 8], out8)
        return out8

    def write(o_ref, out8):
        for t in range(ts):
            o_ref[t, pl.ds(b, 1), :] = out8[t:t + 1]

    qr = q_rows(qb_ref)
    kTf = kb_buf[slot]
    n_blk = past // MOBA_BLOCK
    lane_k = lax.broadcasted_iota(jnp.int32, (D_GRP, LANES), 1)
    km = jnp.zeros((D_GRP, LANES), F32)
    for nblk in range(n_blk):
        km = jnp.where(lane_k == nblk,
                       jnp.mean(kTf[:, nblk * MOBA_BLOCK:(nblk + 1) * MOBA_BLOCK], axis=1, keepdims=True), km)
    gate = jnp.dot(qr, km, precision=HIGHEST, preferred_element_type=F32)
    lane_g = lax.broadcasted_iota(jnp.int32, (rows, LANES), 1)
    rank = jnp.zeros((rows, LANES), F32)
    for mblk in range(n_blk):
        gm = gate[:, mblk:mblk + 1]
        rank = rank + jnp.where((gm > gate) | ((gm == gate) & (mblk < lane_g)), 1.0, 0.0)
    sel = jnp.where((rank < MOBA_TOPK) & (lane_g < n_blk), 1.0, 0.0)
    qrb = qr.astype(BF16)
    s_full = jnp.dot(qrb, kTf.astype(BF16), preferred_element_type=F32)
    s_past = jnp.concatenate(
        [jnp.where(sel[:, nblk:nblk + 1] > 0.5, s_full[:, nblk * MOBA_BLOCK:(nblk + 1) * MOBA_BLOCK], NEG_INF)
         for nblk in range(n_blk)], axis=1)
    k_new = rows8(kb_ref)
    v_new = rows8(vb_ref)
    s_new = lax.dot_general(qrb, k_new.astype(BF16), (((1,), (1,)), ((), ())), preferred_element_type=F32)
    s_new = jnp.where(new_ok, s_new, NEG_INF)
    write(ob_ref, attend(vb_buf[slot].astype(BF16), v_new, s_past, s_new))

    x = lf_buf[slot]
    r = lax.broadcasted_iota(jnp.int32, (page, page), 0)
    c = lax.broadcasted_iota(jnp.int32, (page, page), 1)
    upper = jnp.where(r <= c, 1.0, 0.0).astype(F32)
    within = jnp.dot(x, upper, precision=HIGHEST, preferred_element_type=F32)
    np4 = n_pages * N_HEADS
    rr = lax.broadcasted_iota(jnp.int32, (np4, np4), 0)
    cc = lax.broadcasted_iota(jnp.int32, (np4, np4), 1)
    same_head = (rr % N_HEADS) == (cc % N_HEADS)
    tot = jnp.broadcast_to(within[:, page - 1:page], (np4, page))
    before = jnp.dot(jnp.where(same_head & (cc < rr - rr % N_HEADS), 1.0, 0.0).astype(F32), tot,
                     precision=HIGHEST, preferred_element_type=F32)
    total = jnp.dot(jnp.where(same_head, 1.0, 0.0).astype(F32), tot, precision=HIGHEST, preferred_element_type=F32)
    after = total - (within + before)

    def page_bias(p):
        return jnp.concatenate([jnp.broadcast_to(after[N_HEADS * p + h:N_HEADS * p + h + 1, :], (8, page))
                                for h in range(N_HEADS)], axis=0)

    lf8 = rows8(lf_ref, LANES)
    cum_rows = []
    run = jnp.zeros((1, LANES), F32)
    for t in range(ts):
        run = run + lf8[t:t + 1]
        cum_rows.append(run)
    cnew_cols = jnp.zeros((rows, 8), F32)
    for t in range(ts):
        col_t = jnp.concatenate([jnp.broadcast_to(cum_rows[t][:, h:h + 1], (8, 1)) for h in range(N_HEADS)],
                                axis=0)
        cnew_cols = jnp.where(c_new == t, col_t, cnew_cols)
    qr = q_rows(qc_ref)
    qrb = qr.astype(BF16)
    kTf = kc_buf[slot]
    s_full = jnp.dot(qrb, kTf.astype(BF16), preferred_element_type=F32)
    s_past = jnp.concatenate([s_full[:, p * page:(p + 1) * page] + page_bias(p) for p in range(n_pages)], axis=1)
    k_new = rows8(kc_ref)
    v_new = rows8(vc_ref)
    s_new = lax.dot_general(qrb, k_new.astype(BF16), (((1,), (1,)), ((), ())), preferred_element_type=F32)
    s_new = jnp.where(new_ok, s_new - cnew_cols, NEG_INF)
    write(oc_ref, attend(vc_buf[slot].astype(BF16), v_new, s_past, s_new))


def _attn_sample(page_table, qb, kb, vb, qc, kc, vc, lf, pools, consts, l):
    ns, n_pages = page_table.shape
    ts = consts["ts"]
    pkb, pvb, pkc, pvc, plf = pools
    page = pkb.shape[3]
    past = n_pages * page
    tm3 = lambda a: a.reshape(ts, ns, a.shape[-1])
    whole = lambda w: pl.BlockSpec((ts, ns, w), lambda b, pt: (0, 0, 0))
    anyspec = pl.BlockSpec(memory_space=pl.ANY)
    grid_spec = pltpu.PrefetchScalarGridSpec(
        num_scalar_prefetch=1,
        grid=(ns // SAMPLE_SEQS,),
        in_specs=[whole(D_GRP)] * 6 + [whole(LANES)] + [anyspec] * 5,
        out_specs=[whole(D_GRP), whole(D_GRP)],
        scratch_shapes=[pltpu.VMEM((2 * SAMPLE_SEQS, D_GRP, past), F32)] * 4
                       + [pltpu.VMEM((2 * SAMPLE_SEQS, n_pages * N_HEADS, page), F32),
                          pltpu.SemaphoreType.DMA((2 * SAMPLE_SEQS, 5))],
    )
    ob, oc = pl.pallas_call(
        functools.partial(_attn_sample_kernel, layer=l, n_pages=n_pages),
        grid_spec=grid_spec,
        out_shape=[jax.ShapeDtypeStruct((ts, ns, D_GRP), F32)] * 2,
        compiler_params=pltpu.CompilerParams(dimension_semantics=("arbitrary",), vmem_limit_bytes=VMEM_LIMIT),
        name="attn_sample",
    )(page_table.reshape(-1), tm3(qb), tm3(kb), tm3(vb), tm3(qc), tm3(kc), tm3(vc), tm3(lf),
      pkb, pvb, pkc, pvc, plf)
    return ob.reshape(ts * ns, D_GRP), oc.reshape(ts * ns, D_GRP)


def _ffn_sample_kernel(x_ref, oa_ref, ob_ref, oc_ref, od_ref, wo_ref, gffn_ref, wup_ref, cw_ref, cb_ref, wdn_ref,
                       fbuf_ref, y_ref, tail_ref, *, ts):
    n = x_ref.shape[0]
    ns = n // ts
    dff = wdn_ref.shape[0]

    mix = None
    for g, ref in enumerate((oa_ref, ob_ref, oc_ref, od_ref)):
        part = jnp.dot(ref[...].astype(BF16), wo_ref[g * D_GRP:(g + 1) * D_GRP, :], preferred_element_type=F32)
        mix = part if mix is None else mix + part
    x1 = x_ref[...] + mix
    hf = _rms(x1, gffn_ref[...]).astype(BF16)

    def conv_cols(c0):
        up = jnp.dot(hf, wup_ref[:, c0:c0 + FF_CHUNK], preferred_element_type=F32)
        slabs = [fbuf_ref[r, :, c0:c0 + FF_CHUNK] for r in range(FFN_CONV_W - 1)]
        slabs += [up[t * ns:(t + 1) * ns] for t in range(ts)]
        for r in range(FFN_CONV_W - 1):
            tail_ref[r, :, c0:c0 + FF_CHUNK] = slabs[ts + r]
        outs = []
        for t in range(ts):
            y = cb_ref[:, c0:c0 + FF_CHUNK]
            for tap in range(FFN_CONV_W):
                y = y + cw_ref[tap:tap + 1, c0:c0 + FF_CHUNK] * slabs[t + tap]
            outs.append(y)
        return jnp.concatenate(outs, axis=0)

    acc = None
    for c in range(dff // FF_CHUNK):
        gate = conv_cols(c * FF_CHUNK)
        lin = conv_cols(dff + c * FF_CHUNK)
        act = (gate * jax.nn.sigmoid(gate) * lin).astype(BF16)
        down = jnp.dot(act, wdn_ref[c * FF_CHUNK:(c + 1) * FF_CHUNK, :], preferred_element_type=F32)
        acc = down if acc is None else acc + down
    y_ref[...] = x1 + acc


def _ffn_sample(x_tm, oa, ob, oc, od, fbuf_tm, lp, consts, l):
    n, D = x_tm.shape
    dff = lp["w_down"].shape[1]
    ns = n // consts["ts"]
    layer = lambda *s: pl.BlockSpec((None,) + s, lambda i: (l,) + (0,) * len(s), pipeline_mode=pl.Buffered(1))
    const = lambda *s: pl.BlockSpec(s, lambda i: (0,) * len(s))
    return pl.pallas_call(
        functools.partial(_ffn_sample_kernel, ts=consts["ts"]),
        grid=(1,),
        in_specs=[const(n, D)] + [const(n, D_GRP)] * 4 + [layer(D, D), layer(1, D), layer(D, 2 * dff),
                  layer(8, 2 * dff), layer(1, 2 * dff), layer(dff, D), const(2, ns, 2 * dff)],
        out_specs=[const(n, D), const(2, ns, 2 * dff)],
        out_shape=[jax.ShapeDtypeStruct((n, D), F32), jax.ShapeDtypeStruct((2, ns, 2 * dff), F32)],
        compiler_params=pltpu.CompilerParams(dimension_semantics=("arbitrary",), vmem_limit_bytes=VMEM_LIMIT),
        name="ffn_sample",
    )(x_tm, oa, ob, oc, od, lp["w_out"], lp["norm_ffn"], lp["w_up"], lp["ffn_conv_w"], lp["ffn_conv_b"],
      lp["w_down"], fbuf_tm)


def _pad_rows(a, rows):
    return jnp.pad(a, ((0, 0), (0, rows - a.shape[1]), (0, 0)))


def _rope_tables(pos):
    half = HEAD_DIM // 2
    inv = ROPE_THETA ** (-jnp.arange(half, dtype=F32) / half)
    ang = pos.astype(F32)[:, None] * inv[None, :]
    cos = jnp.tile(jnp.cos(ang), (1, 2 * N_HEADS))
    sin = jnp.sin(ang)
    sin_signed = jnp.tile(jnp.concatenate([-sin, sin], axis=1), (1, N_HEADS))
    return cos, sin_signed


def _prepare(norm_mix, w_in, sgu_norm, sgu_w, sgu_b, moba_qn, moba_kn, fox_qn, fox_kn, fox_fb, conv_w, conv_b,
             conv_ln_g, conv_ln_b, w_out, norm_ffn, w_up, ffn_conv_w, ffn_conv_b, w_down, ts):
    depth = w_in.shape[0]
    n_main = 8 * D_GRP
    w16 = w_in.astype(BF16)
    w_in_p = jnp.concatenate([w16[:, :, :n_main], w16[:, :, n_main + N_HEADS:], w16[:, :, n_main:n_main + N_HEADS],
                              jnp.zeros(w_in.shape[:2] + (LANES - N_HEADS,), BF16)], axis=-1)
    per_head = lambda g: jnp.tile(g, (1, N_HEADS))
    gains = jnp.stack([sgu_norm.reshape(depth, D_GRP), per_head(moba_qn), per_head(moba_kn), per_head(fox_qn),
                       per_head(fox_kn)], axis=1)
    lanes_of_head = lambda a: jnp.repeat(a, HEAD_DIM, axis=-1)
    return {
        "norm_mix": norm_mix[:, None, :],
        "w_in": w_in_p,
        "gains": _pad_rows(gains, 8),
        "fb": jnp.pad(fox_fb, ((0, 0), (0, LANES - N_HEADS)))[:, None, :],
        "sgu_w": sgu_w,
        "sgu_bmix": lanes_of_head(jnp.swapaxes(sgu_b, 1, 2)),
        "sgu_w_s": lanes_of_head(jnp.transpose(sgu_w[:, :, :ts, :ts], (0, 2, 3, 1))),
        "sgu_b_s": lanes_of_head(jnp.swapaxes(sgu_b[:, :, :ts], 1, 2)),
        "conv_w": _pad_rows(conv_w, 32),
        "conv_b": conv_b[:, None, :],
        "conv_ln_g": conv_ln_g[:, None, :],
        "conv_ln_b": conv_ln_b[:, None, :],
        "w_out": w_out.astype(BF16),
        "norm_ffn": norm_ffn[:, None, :],
        "w_up": w_up.astype(BF16),
        "ffn_conv_w": _pad_rows(ffn_conv_w, 8),
        "ffn_conv_b": ffn_conv_b[:, None, :],
        "w_down": w_down.astype(BF16),
    }


def _constants(T, ts, ns, past_len, n_pages):
    blk = np.arange(D_GRP) // HEAD_DIM
    bd = jnp.asarray((blk[:, None] == blk[None, :]).astype(np.float32) / HEAD_DIM, dtype=BF16)
    cos_p, sin_p = _rope_tables(jnp.arange(T, dtype=jnp.int32))
    pos_s = past_len + jnp.repeat(jnp.arange(ts, dtype=jnp.int32), ns)
    cos_s, sin_s = _rope_tables(pos_s)
    return {"bd": bd, "cos_p": cos_p, "sin_p": sin_p, "cos_s": cos_s, "sin_s": sin_s, "ts": ts}


def kernel(x_prompt, x_sample, cache_moba_k, cache_moba_v, cache_fox_k, cache_fox_v, cache_fox_logf, state_conv, state_ffn_conv, page_table, norm_mix, w_in, sgu_norm, sgu_w, sgu_b, moba_qn, moba_kn, fox_qn, fox_kn, fox_fb, conv_w, conv_b, conv_ln_g, conv_ln_b, w_out, norm_ffn, w_up, ffn_conv_w, ffn_conv_b, w_down):
    B, T, D = x_prompt.shape
    ns, ts, _ = x_sample.shape
    depth = w_in.shape[0]
    n_pages = page_table.shape[1]
    page = cache_moba_k.shape[2]
    past_len = n_pages * page
    assert T % ATT_BLOCK == 0 and T % TOKEN_TILE == 0 and T % FFN_TILE == 0
    assert past_len % MOBA_BLOCK == 0 and ts <= 8 and ns % SAMPLE_SEQS == 0

    lp = _prepare(norm_mix, w_in, sgu_norm, sgu_w, sgu_b, moba_qn, moba_kn, fox_qn, fox_kn, fox_fb, conv_w, conv_b,
                  conv_ln_g, conv_ln_b, w_out, norm_ffn, w_up, ffn_conv_w, ffn_conv_b, w_down, ts)
    consts = _constants(T, ts, ns, past_len, n_pages)

    chan_major = lambda c: jnp.transpose(c, (0, 1, 3, 4, 2)).reshape(c.shape[0], c.shape[1], D_GRP, c.shape[2])
    pools = (chan_major(cache_moba_k), chan_major(cache_moba_v), chan_major(cache_fox_k), chan_major(cache_fox_v),
             jnp.transpose(cache_fox_logf, (0, 1, 3, 2)))
    hist_tm = jnp.transpose(state_conv, (0, 2, 1, 3))
    fbuf_tm = jnp.transpose(state_ffn_conv, (0, 2, 1, 3))

    xp = x_prompt
    xs = jnp.transpose(x_sample, (1, 0, 2)).reshape(ts * ns, D)
    P = {k: [] for k in ("conv", "ffn")}
    S = {k: [] for k in ("kb", "vb", "kc", "vc", "lf", "conv", "ffn", "sv")}
    carried = [jnp.zeros((depth, B, D_GRP, T), F32) for _ in range(4)] + [jnp.zeros((depth, B, N_HEADS, T), F32)]
    for l in range(depth):
        carried, (oa, glu, qbT, qcT16, kb16, kc16, vbT16, vcT16, km) = _inproj_prompt(xp, lp, consts, l, carried)
        ob, oc = _attn_prompt(qbT, qcT16, kb16, vbT16, kc16, vcT16, km, carried[4], l)
        od, conv_tail = _conv_prompt(glu, lp, l)
        xp, ffn_tail = _ffn_prompt(xp, oa, ob, oc, od, lp, l)
        P["conv"].append(conv_tail)
        P["ffn"].append(ffn_tail)

        soa, sod, sqb, skb, svb, sqc, skc, svc, slf, ssv, s_tail = _inproj_sample(xs, hist_tm, lp, consts, l)
        sob, soc = _attn_sample(page_table, sqb, skb, svb, sqc, skc, svc, slf, pools, consts, l)
        xs, sffn_tail = _ffn_sample(xs, soa, sob, soc, sod, fbuf_tm[l], lp, consts, l)
        for k, v in zip(("kb", "vb", "kc", "vc", "lf", "conv", "ffn", "sv"),
                        (skb, svb, skc, svc, slf, s_tail, sffn_tail, ssv)):
            S[k].append(v)

    P = {k: jnp.stack(v) for k, v in P.items()}
    P.update(zip(("kb", "vb", "kc", "vc", "lf"), carried))
    S = {k: jnp.stack(v) for k, v in S.items()}
    heads_p = lambda a: jnp.transpose(a.reshape(depth, B, N_HEADS, HEAD_DIM, T), (0, 1, 4, 2, 3))
    heads_s = lambda a: jnp.transpose(a.reshape(depth, ts, ns, N_HEADS, HEAD_DIM), (0, 2, 1, 3, 4))
    return (xp, jnp.transpose(xs.reshape(ts, ns, D), (1, 0, 2)),
            heads_p(P["kb"]), heads_p(P["vb"]), heads_p(P["kc"]), heads_p(P["vc"]),
            jnp.transpose(P["lf"], (0, 1, 3, 2)), P["conv"], P["ffn"],
            heads_s(S["kb"]), heads_s(S["vb"]), heads_s(S["kc"]), heads_s(S["vc"]),
            jnp.transpose(S["lf"].reshape(depth, ts, ns, LANES)[..., :N_HEADS], (0, 2, 1, 3)),
            jnp.transpose(S["conv"], (0, 2, 1, 3)), jnp.transpose(S["ffn"], (0, 2, 1, 3)),
            jnp.transpose(S["sv"].reshape(depth, ts, ns, D_GRP), (0, 2, 1, 3)))
```
